```python
import math
import jax, jax.numpy as jnp
from jax import lax
import numpy as np

D_MODEL = 4096
BATCH = 8
SEQ = 2048
DEPTH = 2
DEC_BATCH = 8
DEC_SEQ = 64
PAST_LEN = 2048

CHUNK = 64
Q_BLOCK = 128
EPS = 1e-6

MLA_HEADS = 16
Q_LORA = 1024
KV_LORA = 512
NOPE_DIM = 128
ROPE_DIM = 64
V_DIM = 128
MLA_WIDTH = MLA_HEADS * V_DIM
ROPE_BASE = 10000.0
MLA_SCALE = (NOPE_DIM + ROPE_DIM) ** -0.5

SSM_HEADS = 32
SSM_HEADDIM = 64
SSM_INNER = SSM_HEADS * SSM_HEADDIM
SSM_GROUPS = 8
D_STATE = 128
CONV_W = 4
CONV_DIM = SSM_INNER + 2 * SSM_GROUPS * D_STATE
SSD_BLOCK = CHUNK

MIX_WIDTH = MLA_WIDTH + SSM_INNER
OFF_CKV = Q_LORA
OFF_KPE = OFF_CKV + KV_LORA
OFF_Z = OFF_KPE + ROPE_DIM
OFF_XBC = OFF_Z + SSM_INNER
OFF_DT = OFF_XBC + CONV_DIM
IN_WIDTH = OFF_DT + SSM_HEADS

N_MEM = 256
XA_HEADS = 4
XA_HEAD_DIM = 128
XA_WIDTH = XA_HEADS * XA_HEAD_DIM
XA_SCALE = XA_HEAD_DIM ** -0.5

D_FF = -(-8 * D_MODEL // (3 * 256)) * 256

kernel_name = "hybrid_mla_ssd_streaming_step"


def rms_norm(x, g):
    xf = x.astype(jnp.float32)
    y = xf * lax.rsqrt(jnp.mean(xf * xf, axis=-1, keepdims=True) + EPS)
    return (y * g.astype(jnp.float32)).astype(x.dtype)


def rope(x, pos):
    half = ROPE_DIM // 2
    inv = ROPE_BASE ** (-jnp.arange(half, dtype=jnp.float32) / half)
    ang = pos.astype(jnp.float32)[:, None] * inv[None, :]
    ang = ang.reshape(ang.shape[:1] + (1,) * (x.ndim - 3) + (half,))
    cos, sin = jnp.cos(ang), jnp.sin(ang)
    xf = x.astype(jnp.float32)
    x1, x2 = xf[..., :half], xf[..., half:]
    return jnp.concatenate([x1 * cos - x2 * sin, x1 * sin + x2 * cos], -1).astype(x.dtype)


def chunk_causal_attention(q, k, v, q_pos, k_pos, scale):
    Bsz, Sq, H, Dk = q.shape
    blk = min(Q_BLOCK, Sq)
    nb = Sq // blk
    qb = q.reshape(Bsz, nb, blk, H, Dk).transpose(1, 0, 2, 3, 4)
    pb = q_pos.reshape(nb, blk)
    k_chunk = k_pos // CHUNK

    def one_block(args):
        qi, pi = args
        s = jnp.einsum('bqhd,bkhd->bhqk', qi, k, preferred_element_type=jnp.float32) * scale
        mask = k_chunk[None, :] <= (pi // CHUNK)[:, None]
        s = jnp.where(mask[None, None], s, -jnp.inf)
        p = jax.nn.softmax(s, axis=-1).astype(v.dtype)
        return jnp.einsum('bhqk,bkhd->bqhd', p, v)

    out = lax.map(one_block, (qb, pb))
    return out.transpose(1, 0, 2, 3, 4).reshape(Bsz, Sq, H, v.shape[-1])


def ssd_scan(x, dt, A, Bm, Cm, h0):
    f32 = jnp.float32
    Bsz, L, H, P = x.shape
    G, N = Bm.shape[2], Bm.shape[3]
    R = H // G
    Q = min(SSD_BLOCK, L)
    C = L // Q
    xdt = (x.astype(f32) * dt[..., None]).reshape(Bsz, C, Q, G, R, P)
    a_cum = jnp.cumsum((dt * A).reshape(Bsz, C, Q, G, R), axis=2)
    Bc = Bm.astype(f32).reshape(Bsz, C, Q, G, N)
    Cc = Cm.astype(f32).reshape(Bsz, C, Q, G, N)
    seg = a_cum[:, :, :, None] - a_cum[:, :, None, :]
    causal = jnp.tril(jnp.ones((Q, Q), dtype=bool))
    decay = jnp.exp(jnp.where(causal[None, None, :, :, None, None], seg, -jnp.inf))
    cb = jnp.einsum('bcign,bcjgn->bcijg', Cc, Bc)
    y_diag = jnp.einsum('bcijg,bcijgr,bcjgrp->bcigrp', cb, decay, xdt)
    decay_end = jnp.exp(a_cum[:, :, -1:] - a_cum)
    states = jnp.einsum('bcjgn,bcjgr,bcjgrp->bcgrpn', Bc, decay_end, xdt)
    block_decay = jnp.exp(a_cum[:, :, -1])

    def step(h, inp):
        s, d = inp
        return d[..., None, None] * h + s, h

    h_init = h0.astype(f32).reshape(Bsz, G, R, P, N)
    h_final, h_in = lax.scan(step, h_init, (states.transpose(1, 0, 2, 3, 4, 5), block_decay.transpose(1, 0, 2, 3)))
    h_in = h_in.transpose(1, 0, 2, 3, 4, 5)
    y_off = jnp.einsum('bcign,bcigr,bcgrpn->bcigrp', Cc, jnp.exp(a_cum), h_in)
    y = (y_diag + y_off).reshape(Bsz, L, H, P)
    return y.astype(x.dtype), h_final.reshape(Bsz, H, P, N).astype(h0.dtype)


def token_mixer(h, pos, ckv_past, kpe_past, conv_past, ssm_past, lp):
    Bsz, L, _ = h.shape
    proj = h @ lp['w_in']
    c_q, c_kv, k_pe, z, xbc, dt = jnp.split(proj, [OFF_CKV, OFF_KPE, OFF_Z, OFF_XBC, OFF_DT], axis=-1)

    q = (rms_norm(c_q, lp['q_norm']) @ lp['w_uq']).reshape(Bsz, L, MLA_HEADS, NOPE_DIM + ROPE_DIM)
    q = jnp.concatenate([rms_norm(q[..., :NOPE_DIM], lp['qn_nope']),
                         rope(rms_norm(q[..., NOPE_DIM:], lp['qn_pe']), pos)], axis=-1)
    ckv_new = rms_norm(c_kv, lp['kv_norm'])
    kpe_new = rope(rms_norm(k_pe, lp['kn_pe']), pos)
    ckv_all = jnp.concatenate([ckv_past, ckv_new], axis=1)
    kpe_all = jnp.concatenate([kpe_past, kpe_new], axis=1)
    Sk = ckv_all.shape[1]
    kv = (ckv_all @ lp['w_ukv']).reshape(Bsz, Sk, MLA_HEADS, NOPE_DIM + V_DIM)
    k = jnp.concatenate([rms_norm(kv[..., :NOPE_DIM], lp['kn_nope']),
                         jnp.broadcast_to(kpe_all[:, :, None, :], (Bsz, Sk, MLA_HEADS, ROPE_DIM))], axis=-1)
    k_pos = jnp.concatenate([jnp.arange(Sk - L, dtype=jnp.int32), pos])
    attn = chunk_causal_attention(q, k, kv[..., NOPE_DIM:], pos, k_pos, MLA_SCALE).reshape(Bsz, L, MLA_WIDTH)

    xbc_all = jnp.concatenate([conv_past, xbc], axis=1)
    conv_new = xbc_all[:, -(CONV_W - 1):]
    conv = sum((xbc_all[:, i:i + L] * lp['conv_w'][i] for i in range(CONV_W)), lp['conv_b'])
    xbc_c = jax.nn.silu(conv)
    xs, Bm, Cm = jnp.split(xbc_c, [SSM_INNER, SSM_INNER + SSM_GROUPS * D_STATE], axis=-1)
    dt = jax.nn.softplus(dt.astype(jnp.float32) + lp['dt_bias'].astype(jnp.float32))
    A = -jnp.exp(lp['a_log'].astype(jnp.float32))
    xs_h = xs.reshape(Bsz, L, SSM_HEADS, SSM_HEADDIM)
    y, ssm_new = ssd_scan(xs_h, dt, A,
                          Bm.reshape(Bsz, L, SSM_GROUPS, D_STATE),
                          Cm.reshape(Bsz, L, SSM_GROUPS, D_STATE), ssm_past)
    y = (y + lp['d_skip'][:, None] * xs_h).reshape(Bsz, L, SSM_INNER) * jax.nn.silu(z)
    y = rms_norm(y.reshape(Bsz, L, SSM_GROUPS, SSM_INNER // SSM_GROUPS),
                 lp['ssm_norm'].reshape(SSM_GROUPS, -1)).reshape(Bsz, L, SSM_INNER)

    out = jnp.concatenate([attn, y], axis=-1) @ lp['w_o']
    return out, ckv_new, kpe_new, conv_new, ssm_new


def memory_kv(mem, lp):
    Bsz, M, _ = mem.shape
    mn = rms_norm(mem, lp['mem_norm'])
    k = rms_norm((mn @ lp['w_xk']).reshape(Bsz, M, XA_HEADS, XA_HEAD_DIM), lp['xa_norm_k'])
    v = (mn @ lp['w_xv']).reshape(Bsz, M, XA_HEADS, XA_HEAD_DIM)
    return k, v


def cross_attention(h, mem_k, mem_v, lp):
    Bsz, L, _ = h.shape
    q = rms_norm((h @ lp['w_xq']).reshape(Bsz, L, XA_HEADS, XA_HEAD_DIM), lp['xa_norm_q'])
    s = jnp.einsum('blhd,bmhd->bhlm', q, mem_k, preferred_element_type=jnp.float32) * XA_SCALE
    p = jax.nn.softmax(s, axis=-1).astype(mem_v.dtype)
    o = jnp.einsum('bhlm,bmhd->blhd', p, mem_v).reshape(Bsz, L, XA_WIDTH)
    return o @ lp['w_xo']


def swiglu(h, lp):
    return (jax.nn.silu(h @ lp['w_gate']) * (h @ lp['w_up'])) @ lp['w_down']


def block(x, pos, ckv_past, kpe_past, conv_past, ssm_past, mem_k, mem_v, lp):
    mix, ckv_new, kpe_new, conv_new, ssm_new = token_mixer(
        rms_norm(x, lp['norm_mix']), pos, ckv_past, kpe_past, conv_past, ssm_past, lp)
    x = x + mix
    x = x + cross_attention(rms_norm(x, lp['norm_xa']), mem_k, mem_v, lp)
    x = x + swiglu(rms_norm(x, lp['norm_ffn']), lp)
    return x, ckv_new, kpe_new, conv_new, ssm_new


def setup_inputs(seed: int = 0) -> dict:
    key = jax.random.key(seed)
    keys = jax.random.split(key, 64)
    counter = [0]
    f32 = jnp.float32

    def nk():
        k = keys[counter[0]]
        counter[0] += 1
        return k

    def nrm(shape, scale=1.0):
        return jax.random.normal(nk(), shape, f32) * scale

    def gain(shape):
        return 1.0 + 0.02 * jax.random.normal(nk(), shape, f32)

    Lz = DEPTH
    inp = {}
    inp['x_prompt'] = nrm((BATCH, SEQ, D_MODEL))
    inp['x_sample'] = nrm((DEC_BATCH, DEC_SEQ, D_MODEL))
    inp['mem_prompt'] = nrm((BATCH, N_MEM, D_MODEL))
    inp['cache_ckv'] = nrm((Lz, DEC_BATCH, PAST_LEN, KV_LORA))
    inp['cache_kpe'] = nrm((Lz, DEC_BATCH, PAST_LEN, ROPE_DIM))
    inp['state_conv'] = nrm((Lz, DEC_BATCH, CONV_W - 1, CONV_DIM))
    inp['state_ssm'] = nrm((Lz, DEC_BATCH, SSM_HEADS, SSM_HEADDIM, D_STATE), 0.5)
    inp['cache_mem_k'] = nrm((Lz, DEC_BATCH, N_MEM, XA_HEADS, XA_HEAD_DIM))
    inp['cache_mem_v'] = nrm((Lz, DEC_BATCH, N_MEM, XA_HEADS, XA_HEAD_DIM))
    inp['norm_mix'] = gain((Lz, D_MODEL))
    inp['w_in'] = nrm((Lz, D_MODEL, IN_WIDTH), D_MODEL ** -0.5)
    inp['q_norm'] = gain((Lz, Q_LORA))
    inp['w_uq'] = nrm((Lz, Q_LORA, MLA_HEADS * (NOPE_DIM + ROPE_DIM)), Q_LORA ** -0.5)
    inp['kv_norm'] = gain((Lz, KV_LORA))
    inp['w_ukv'] = nrm((Lz, KV_LORA, MLA_HEADS * (NOPE_DIM + V_DIM)), KV_LORA ** -0.5)
    inp['qn_nope'] = gain((Lz, NOPE_DIM))
    inp['qn_pe'] = gain((Lz, ROPE_DIM))
    inp['kn_nope'] = gain((Lz, NOPE_DIM))
    inp['kn_pe'] = gain((Lz, ROPE_DIM))
    inp['conv_w'] = nrm((Lz, CONV_W, CONV_DIM), CONV_W ** -0.5)
    inp['conv_b'] = nrm((Lz, CONV_DIM), 0.02)
    dt0 = jnp.exp(jax.random.uniform(nk(), (Lz, SSM_HEADS), f32, math.log(1e-3), math.log(1e-1)))
    inp['dt_bias'] = dt0 + jnp.log(-jnp.expm1(-dt0))
    inp['a_log'] = jnp.log(jax.random.uniform(nk(), (Lz, SSM_HEADS), f32, 1.0, 16.0))
    inp['d_skip'] = gain((Lz, SSM_HEADS))
    inp['ssm_norm'] = gain((Lz, SSM_INNER))
    inp['w_o'] = nrm((Lz, MIX_WIDTH, D_MODEL), MIX_WIDTH ** -0.5)
    inp['norm_xa'] = gain((Lz, D_MODEL))
    inp['mem_norm'] = gain((Lz, D_MODEL))
    inp['w_xq'] = nrm((Lz, D_MODEL, XA_WIDTH), D_MODEL ** -0.5)
    inp['w_xk'] = nrm((Lz, D_MODEL, XA_WIDTH), D_MODEL ** -0.5)
    inp['w_xv'] = nrm((Lz, D_MODEL, XA_WIDTH), D_MODEL ** -0.5)
    inp['xa_norm_q'] = gain((Lz, XA_HEAD_DIM))
    inp['xa_norm_k'] = gain((Lz, XA_HEAD_DIM))
    inp['w_xo'] = nrm((Lz, XA_WIDTH, D_MODEL), XA_WIDTH ** -0.5)
    inp['norm_ffn'] = gain((Lz, D_MODEL))
    inp['w_gate'] = nrm((Lz, D_MODEL, D_FF), D_MODEL ** -0.5)
    inp['w_up'] = nrm((Lz, D_MODEL, D_FF), D_MODEL ** -0.5)
    inp['w_down'] = nrm((Lz, D_FF, D_MODEL), D_FF ** -0.5)
    return inp


def reference(x_prompt, x_sample, mem_prompt, cache_ckv, cache_kpe, state_conv, state_ssm,
              cache_mem_k, cache_mem_v,
              norm_mix, w_in, q_norm, w_uq, kv_norm, w_ukv, qn_nope, qn_pe, kn_nope, kn_pe,
              conv_w, conv_b, dt_bias, a_log, d_skip, ssm_norm, w_o,
              norm_xa, mem_norm, w_xq, w_xk, w_xv, xa_norm_q, xa_norm_k, w_xo,
              norm_ffn, w_gate, w_up, w_down):
    Bp, S, _ = x_prompt.shape
    Bs, L, _ = x_sample.shape
    past = cache_ckv.shape[2]
    dtype = x_prompt.dtype
    pos_p = jnp.arange(S, dtype=jnp.int32)
    pos_s = past + jnp.arange(L, dtype=jnp.int32)
    ckv0 = jnp.zeros((Bp, 0, KV_LORA), dtype)
    kpe0 = jnp.zeros((Bp, 0, ROPE_DIM), dtype)
    conv0 = jnp.zeros((Bp, CONV_W - 1, CONV_DIM), dtype)
    ssm0 = jnp.zeros((Bp, SSM_HEADS, SSM_HEADDIM, D_STATE), dtype)

    hp, hs = x_prompt, x_sample
    ckv_p, kpe_p, conv_p, ssm_p, mk_p, mv_p = [], [], [], [], [], []
    ckv_s, kpe_s, conv_s, ssm_s = [], [], [], []
    for l in range(DEPTH):
        lp = dict(norm_mix=norm_mix[l], w_in=w_in[l], q_norm=q_norm[l], w_uq=w_uq[l],
                  kv_norm=kv_norm[l], w_ukv=w_ukv[l], qn_nope=qn_nope[l], qn_pe=qn_pe[l],
                  kn_nope=kn_nope[l], kn_pe=kn_pe[l], conv_w=conv_w[l], conv_b=conv_b[l],
                  dt_bias=dt_bias[l], a_log=a_log[l], d_skip=d_skip[l], ssm_norm=ssm_norm[l],
                  w_o=w_o[l], norm_xa=norm_xa[l], mem_norm=mem_norm[l], w_xq=w_xq[l],
                  w_xk=w_xk[l], w_xv=w_xv[l], xa_norm_q=xa_norm_q[l], xa_norm_k=xa_norm_k[l],
                  w_xo=w_xo[l], norm_ffn=norm_ffn[l], w_gate=w_gate[l], w_up=w_up[l],
                  w_down=w_down[l])
        mk, mv = memory_kv(mem_prompt, lp)
        hp, a, b, c, d = block(hp, pos_p, ckv0, kpe0, conv0, ssm0, mk, mv, lp)
        ckv_p.append(a); kpe_p.append(b); conv_p.append(c); ssm_p.append(d)
        mk_p.append(mk); mv_p.append(mv)
        hs, a, b, c, d = block(hs, pos_s, cache_ckv[l], cache_kpe[l], state_conv[l], state_ssm[l],
                               cache_mem_k[l], cache_mem_v[l], lp)
        ckv_s.append(a); kpe_s.append(b); conv_s.append(c); ssm_s.append(d)

    return (hp, hs,
            jnp.stack(ckv_p), jnp.stack(kpe_p), jnp.stack(conv_p), jnp.stack(ssm_p),
            jnp.stack(mk_p), jnp.stack(mv_p),
            jnp.stack(ckv_s), jnp.stack(kpe_s), jnp.stack(conv_s), jnp.stack(ssm_s))
```

```python
import functools

import numpy as np
import jax
import jax.numpy as jnp
from jax import lax
from jax.experimental import pallas as pl
from jax.experimental.pallas import tpu as pltpu

F32 = jnp.float32
BF16 = jnp.bfloat16
HIGHEST = lax.Precision.HIGHEST

EPS = 1e-6
CHUNK = 64
CHUNK_SHIFT = 6
NOPE_DIM = 128
ROPE_DIM = 64
V_DIM = 128
QK_SLAB = 256
ROPE_BASE = 10000.0
MLA_SCALE = (NOPE_DIM + ROPE_DIM) ** -0.5
SSM_HEADDIM = 64
D_STATE = 128
CONV_W = 4
XA_HEAD_DIM = 128
XA_SCALE = XA_HEAD_DIM ** -0.5
LANE = 128
VMEM_LIMIT = 56 * 1024 * 1024


def _round_up(n, m):
    return -(-n // m) * m


def _tile(n, pref, mult=8):
    if n <= pref:
        return n
    t = pref - pref % mult
    while t >= mult:
        if n % t == 0:
            return t
        t -= mult
    return n


def _params(*sem):
    return pltpu.CompilerParams(dimension_semantics=sem, vmem_limit_bytes=VMEM_LIMIT)


def _rms(x, width):
    return lax.rsqrt(jnp.sum(x * x, axis=-1, keepdims=True) * (1.0 / width) + EPS)


def _silu(x):
    return x / (1.0 + jnp.exp(-x))


def _rmsnorm_kernel(x_ref, g_ref, o_ref):
    x = x_ref[...]
    o_ref[...] = (x * _rms(x, x.shape[-1]) * g_ref[...]).astype(o_ref.dtype)


def rmsnorm_bf16(x, g):
    m, d = x.shape
    tm = _tile(m, 256)
    return pl.pallas_call(
        _rmsnorm_kernel,
        grid=(m // tm,),
        in_specs=[pl.BlockSpec((tm, d), lambda i: (i, 0)), pl.BlockSpec((1, d), lambda i: (0, 0))],
        out_specs=pl.BlockSpec((tm, d), lambda i: (i, 0)),
        out_shape=jax.ShapeDtypeStruct((m, d), BF16),
        compiler_params=_params("parallel"),
        name="rmsnorm_bf16",
    )(x, g.reshape(1, d))


def _mm_kernel(*refs, nk, has_res):
    if has_res:
        a_ref, b_ref, r_ref, o_ref = refs[:4]
    else:
        a_ref, b_ref, o_ref = refs[:3]
        r_ref = None
    part = jnp.dot(a_ref[...], b_ref[...], preferred_element_type=F32)

    def finish(total):
        if has_res:
            total = total + r_ref[...]
        o_ref[...] = total.astype(o_ref.dtype)

    if nk == 1:
        finish(part)
        return
    acc_ref = refs[-1]
    k = pl.program_id(2)

    @pl.when(k == 0)
    def _():
        acc_ref[...] = part

    @pl.when(jnp.logical_and(k > 0, k < nk - 1))
    def _():
        acc_ref[...] += part

    @pl.when(k == nk - 1)
    def _():
        finish(acc_ref[...] + part)


def matmul(a, b, *, out_dtype, residual=None, tm=1024, tn=1024, tk=None):
    m, kd = a.shape
    _, n = b.shape
    tm = _tile(m, tm)
    tn = _tile(n, tn, LANE)
    tk = kd if tk is None else _tile(kd, tk, LANE)
    nk = kd // tk
    in_specs = [pl.BlockSpec((tm, tk), lambda i, j, k: (i, k)),
                pl.BlockSpec((tk, tn), lambda i, j, k: (k, j))]
    args = [a, b]
    if residual is not None:
        in_specs.append(pl.BlockSpec((tm, tn), lambda i, j, k: (i, j)))
        args.append(residual)
    scratch = [pltpu.VMEM((tm, tn), F32)] if nk > 1 else []
    return pl.pallas_call(
        functools.partial(_mm_kernel, nk=nk, has_res=residual is not None),
        grid=(m // tm, n // tn, nk),
        in_specs=in_specs,
        out_specs=pl.BlockSpec((tm, tn), lambda i, j, k: (i, j)),
        out_shape=jax.ShapeDtypeStruct((m, n), out_dtype),
        scratch_shapes=scratch,
        compiler_params=_params("parallel", "parallel", "arbitrary"),
        name="matmul",
    )(*args)


def _swiglu_kernel(a_ref, wg_ref, wu_ref, o_ref):
    a = a_ref[...]
    g = jnp.dot(a, wg_ref[...], preferred_element_type=F32)
    u = jnp.dot(a, wu_ref[...], preferred_element_type=F32)
    o_ref[...] = (_silu(g) * u).astype(o_ref.dtype)


def swiglu_up(a, wg, wu, *, tm=1024, tn=512):
    m, d = a.shape
    f = wg.shape[1]
    tm = _tile(m, tm)
    tn = _tile(f, tn, LANE)
    return pl.pallas_call(
        _swiglu_kernel,
        grid=(m // tm, f // tn),
        in_specs=[pl.BlockSpec((tm, d), lambda i, j: (i, 0)),
                  pl.BlockSpec((d, tn), lambda i, j: (0, j)),
                  pl.BlockSpec((d, tn), lambda i, j: (0, j))],
        out_specs=pl.BlockSpec((tm, tn), lambda i, j: (i, j)),
        out_shape=jax.ShapeDtypeStruct((m, f), BF16),
        compiler_params=_params("parallel", "parallel"),
        name="swiglu_up",
    )(a, wg, wu)


def _q_kernel(cq_ref, gq_ref, w_ref, gn_ref, tab_ref, q_ref, *, heads):
    x = cq_ref[...]
    xn = (x * _rms(x, x.shape[-1]) * gq_ref[...]).astype(BF16)
    tab = tab_ref[...]
    gn = gn_ref[...]
    for h in range(heads):
        acc = jnp.dot(xn, w_ref[:, h * QK_SLAB:(h + 1) * QK_SLAB], preferred_element_type=F32)
        nope = acc[:, :NOPE_DIM]
        rot = acc[:, NOPE_DIM:]
        qn = nope * _rms(nope, NOPE_DIM) * gn
        qr = rot * _rms(rot, 2 * ROPE_DIM) * tab
        q_ref[h, :, :NOPE_DIM] = (qn * MLA_SCALE).astype(BF16)
        q_ref[h, :, NOPE_DIM:] = (qr * MLA_SCALE).astype(BF16)


def q_proj(proj, col_blk, q_lora, gq, w_uq_r, gn, tab, heads):
    m = proj.shape[0]
    s_tab = tab.shape[0]
    tm = _tile(s_tab, 256)
    nt = s_tab // tm
    return pl.pallas_call(
        functools.partial(_q_kernel, heads=heads),
        grid=(m // tm,),
        in_specs=[pl.BlockSpec((tm, q_lora), lambda i: (i, col_blk)),
                  pl.BlockSpec((1, q_lora), lambda i: (0, 0)),
                  pl.BlockSpec((q_lora, heads * QK_SLAB), lambda i: (0, 0)),
                  pl.BlockSpec((1, NOPE_DIM), lambda i: (0, 0)),
                  pl.BlockSpec((tm, 2 * ROPE_DIM), lambda i: (i % nt, 0))],
        out_specs=pl.BlockSpec((heads, tm, QK_SLAB), lambda i: (0, i, 0)),
        out_shape=jax.ShapeDtypeStruct((heads, m, QK_SLAB), BF16),
        compiler_params=_params("parallel"),
        name="q_proj",
    )(proj, gq.reshape(1, -1), w_uq_r, gn.reshape(1, -1), tab)


def _kv_norm_kernel(ckv_ref, kpe_ref, g_ref, tab_ref, ckv_o, kpe_o):
    x = ckv_ref[...]
    ckv_o[...] = x * _rms(x, x.shape[-1]) * g_ref[...]
    t = kpe_ref[...]
    u = t * _rms(t, 2 * ROPE_DIM) * tab_ref[...]
    kpe_o[...] = u[:, :ROPE_DIM] + u[:, ROPE_DIM:]


def kv_norm(proj, ckv_blk, kv_lora, kpe_blk, g, tab):
    m = proj.shape[0]
    s_tab = tab.shape[0]
    tm = _tile(s_tab, 512)
    nt = s_tab // tm
    return pl.pallas_call(
        _kv_norm_kernel,
        grid=(m // tm,),
        in_specs=[pl.BlockSpec((tm, kv_lora), lambda i: (i, ckv_blk)),
                  pl.BlockSpec((tm, 2 * ROPE_DIM), lambda i: (i, kpe_blk)),
                  pl.BlockSpec((1, kv_lora), lambda i: (0, 0)),
                  pl.BlockSpec((tm, 2 * ROPE_DIM), lambda i: (i % nt, 0))],
        out_specs=[pl.BlockSpec((tm, kv_lora), lambda i: (i, 0)),
                   pl.BlockSpec((tm, ROPE_DIM), lambda i: (i, 0))],
        out_shape=[jax.ShapeDtypeStruct((m, kv_lora), F32),
                   jax.ShapeDtypeStruct((m, ROPE_DIM), F32)],
        compiler_params=_params("parallel"),
        name="kv_norm",
    )(proj, proj, g.reshape(1, -1), tab)


def _kv_up_kernel(ckv_ref, kpe_ref, w_ref, gn_ref, k_ref, v_ref, *, heads):
    xb = ckv_ref[...].astype(BF16)
    kp = kpe_ref[...]
    kp2 = jnp.concatenate([kp, kp], axis=-1).astype(BF16)
    gn = gn_ref[...]
    for h in range(heads):
        acc = jnp.dot(xb, w_ref[:, h * 256:(h + 1) * 256], preferred_element_type=F32)
        nope = acc[:, :NOPE_DIM]
        k_ref[h, :, :NOPE_DIM] = (nope * _rms(nope, NOPE_DIM) * gn).astype(BF16)
        k_ref[h, :, NOPE_DIM:] = kp2
        v_ref[h] = acc[:, NOPE_DIM:].astype(BF16)


def kv_up(ckv, kpe, w_ukv, gn, heads):
    m, kv_lora = ckv.shape
    tm = _tile(m, 256)
    return pl.pallas_call(
        functools.partial(_kv_up_kernel, heads=heads),
        grid=(m // tm,),
        in_specs=[pl.BlockSpec((tm, kv_lora), lambda i: (i, 0)),
                  pl.BlockSpec((tm, ROPE_DIM), lambda i: (i, 0)),
                  pl.BlockSpec((kv_lora, heads * 256), lambda i: (0, 0)),
                  pl.BlockSpec((1, NOPE_DIM), lambda i: (0, 0))],
        out_specs=[pl.BlockSpec((heads, tm, QK_SLAB), lambda i: (0, i, 0)),
                   pl.BlockSpec((heads, tm, V_DIM), lambda i: (0, i, 0))],
        out_shape=[jax.ShapeDtypeStruct((heads, m, QK_SLAB), BF16),
                   jax.ShapeDtypeStruct((heads, m, V_DIM), BF16)],
        compiler_params=_params("parallel"),
        name="kv_up",
    )(ckv, kpe, w_ukv, gn.reshape(1, -1))


def _flash_kernel(q_ref, k_ref, v_ref, o_ref, m_ref, l_ref, acc_ref, *, heads, tq, tk, nk, q_off):
    i = pl.program_id(1)
    j = pl.program_id(2)
    q_lo = q_off + i * tq
    j_last = jnp.minimum((((q_lo + tq - 1) // CHUNK + 1) * CHUNK - 1) // tk, nk - 1)
    fully_visible = ((j * tk + tk - 1) // CHUNK) <= (q_lo // CHUNK)

    @pl.when(j == 0)
    def _():
        m_ref[...] = jnp.full(m_ref.shape, -jnp.inf, F32)
        l_ref[...] = jnp.zeros(l_ref.shape, F32)
        acc_ref[...] = jnp.zeros(acc_ref.shape, F32)

    def tile(masked):
        if masked:
            q_chunk = (q_lo + lax.broadcasted_iota(jnp.int32, (tq, tk), 0)) >> CHUNK_SHIFT
            k_chunk = (j * tk + lax.broadcasted_iota(jnp.int32, (tq, tk), 1)) >> CHUNK_SHIFT
            visible = k_chunk <= q_chunk

        def head(h, carry):
            s = lax.dot_general(q_ref[h], k_ref[h], (((1,), (1,)), ((), ())),
                                preferred_element_type=F32)
            if masked:
                s = jnp.where(visible, s, -jnp.inf)
            m_prev = m_ref[h]
            m_new = jnp.maximum(m_prev, jnp.max(s, axis=-1, keepdims=True))
            alpha = jnp.exp(m_prev - m_new)
            p = jnp.exp(s - m_new)
            l_ref[h] = alpha * l_ref[h] + jnp.sum(p, axis=-1, keepdims=True)
            acc_ref[h] = alpha * acc_ref[h] + jnp.dot(p.astype(BF16), v_ref[h],
                                                      preferred_element_type=F32)
            m_ref[h] = m_new
            return carry

        lax.fori_loop(0, heads, head, 0)

    @pl.when(jnp.logical_and(j <= j_last, fully_visible))
    def _():
        tile(False)

    @pl.when(jnp.logical_and(j <= j_last, jnp.logical_not(fully_visible)))
    def _():
        tile(True)

    @pl.when(j == nk - 1)
    def _():
        for h in range(heads):
            o_ref[:, h * V_DIM:(h + 1) * V_DIM] = (acc_ref[h] / l_ref[h]).astype(o_ref.dtype)


def flash_attention(q, k, v, batch, *, tq=512, tk=512):
    heads, mq, _ = q.shape
    mk = k.shape[1]
    sq, sk = mq // batch, mk // batch
    tq = _tile(sq, tq)
    tk = _tile(sk, tk, 16)
    nq, nk = sq // tq, sk // tk
    q_off = sk - sq

    def kv_index(b, i, j):
        j_last = (((q_off + i * tq + tq - 1) // CHUNK + 1) * CHUNK - 1) // tk
        return (0, b * nk + jnp.minimum(j, jnp.minimum(j_last, nk - 1)), 0)

    return pl.pallas_call(
        functools.partial(_flash_kernel, heads=heads, tq=tq, tk=tk, nk=nk, q_off=q_off),
        grid=(batch, nq, nk),
        in_specs=[pl.BlockSpec((heads, tq, QK_SLAB), lambda b, i, j: (0, b * nq + i, 0)),
                  pl.BlockSpec((heads, tk, QK_SLAB), kv_index),
                  pl.BlockSpec((heads, tk, V_DIM), kv_index)],
        out_specs=pl.BlockSpec((tq, heads * V_DIM), lambda b, i, j: (b * nq + i, 0)),
        out_shape=jax.ShapeDtypeStruct((mq, heads * V_DIM), BF16),
        scratch_shapes=[pltpu.VMEM((heads, tq, 1), F32),
                        pltpu.VMEM((heads, tq, 1), F32),
                        pltpu.VMEM((heads, tq, V_DIM), F32)],
        compiler_params=_params("parallel", "parallel", "arbitrary"),
        name="flash_attention",
    )(q, k, v)


def _ssd_kernel(xbc_ref, z_ref, dt_ref, cinit_ref, hinit_ref, cw_ref, cb_ref, dtb_ref, alog_ref,
                dsk_ref, gn_ref, e_ref, y_ref, hout_ref, buf, ht, *, groups, rep, nc):
    c = pl.program_id(1)
    q = CHUNK
    gw = rep * SSM_HEADDIM
    hp = groups * gw
    halo = CONV_W - 1
    base = 8

    @pl.when(c == 0)
    def _():
        buf[base - halo:base, :] = cinit_ref[0]
        ht[...] = hinit_ref[0]

    buf[base:base + q, :] = xbc_ref[...]
    conv = cb_ref[...]
    for i in range(CONV_W):
        conv = conv + buf[base - halo + i:base - halo + i + q, :] * cw_ref[i:i + 1, :]
    tail = buf[base + q - halo:base + q, :]
    buf[base - halo:base, :] = tail
    xc = _silu(conv)
    xs = xc[:, :hp]
    bm = xc[:, hp:hp + groups * D_STATE].astype(BF16)
    cm = xc[:, hp + groups * D_STATE:].astype(BF16)

    dt_in = dt_ref[...] + dtb_ref[...]
    dt = jnp.maximum(dt_in, 0.0) + jnp.log1p(jnp.exp(-jnp.abs(dt_in)))
    a = dt * (-jnp.exp(alog_ref[...]))
    row = lax.broadcasted_iota(jnp.int32, (q, q), 0)
    col = lax.broadcasted_iota(jnp.int32, (q, q), 1)
    tri = (col <= row).astype(F32)
    a_cum = jnp.dot(tri, a, precision=HIGHEST, preferred_element_type=F32)
    a_col = jnp.dot(a_cum, e_ref[...], precision=HIGHEST, preferred_element_type=F32)
    dt_col = jnp.dot(dt, e_ref[...], precision=HIGHEST, preferred_element_type=F32)

    r_idx = lax.broadcasted_iota(jnp.int32, (q, hp), 0)
    l_idx = lax.broadcasted_iota(jnp.int32, (q, hp), 1) & (q - 1)
    a_row = jnp.sum(jnp.where(l_idx == r_idx, a_col, 0.0), axis=0, keepdims=True)
    decay = jnp.exp(jnp.where(l_idx <= r_idx, a_col - a_row, -jnp.inf))
    a_last = a_col[q - 1:q, :]
    exp_a = jnp.exp(a_col)
    decay_end = jnp.exp(a_last - a_col)
    block_decay = jnp.exp(a_last)

    xdt = xs * dt_col
    xw = (xdt * decay_end).astype(BF16)
    y_skip = dsk_ref[...] * xs
    zg = _silu(z_ref[...])

    rr = lax.broadcasted_iota(jnp.int32, (rep * q, gw), 0) >> CHUNK_SHIFT
    cc = lax.broadcasted_iota(jnp.int32, (rep * q, gw), 1) >> CHUNK_SHIFT
    diag_blocks = rr == cc

    for g in range(groups):
        lanes = slice(g * gw, (g + 1) * gw)
        bg = bm[:, g * D_STATE:(g + 1) * D_STATE]
        cg = cm[:, g * D_STATE:(g + 1) * D_STATE]
        b_rep = jnp.concatenate([bg] * rep, axis=0)
        cb = lax.dot_general(cg, b_rep, (((1,), (1,)), ((), ())), preferred_element_type=F32)
        mg = (cb * decay[:, lanes]).astype(BF16)
        xdt_g = xdt[:, lanes]
        x_blk = jnp.where(diag_blocks, jnp.concatenate([xdt_g] * rep, axis=0), 0.0).astype(BF16)
        y_diag = jnp.dot(mg, x_blk, preferred_element_type=F32)
        h_in = ht[g]
        y_off = jnp.dot(cg, h_in.astype(BF16), preferred_element_type=F32) * exp_a[:, lanes]
        st = lax.dot_general(bg, xw[:, lanes], (((0,), (0,)), ((), ())),
                             preferred_element_type=F32)
        ht[g] = h_in * block_decay[:, lanes] + st
        yg = (y_diag + y_off + y_skip[:, lanes]) * zg[:, lanes]
        y_ref[:, lanes] = (yg * _rms(yg, gw) * gn_ref[:, lanes]).astype(y_ref.dtype)

    @pl.when(c == nc - 1)
    def _():
        hout_ref[0] = ht[...]


def ssd_mixer(proj, batch, xbc_blk, conv_dim, z_blk, hp, dt_blk, conv_init, h_init_t, conv_w,
              conv_b, dtb, alog, dsk_e, gn, e_mat, groups, rep):
    m = proj.shape[0]
    nc = m // batch // CHUNK
    gw = rep * SSM_HEADDIM
    const2 = lambda b, c: (0, 0)
    return pl.pallas_call(
        functools.partial(_ssd_kernel, groups=groups, rep=rep, nc=nc),
        grid=(batch, nc),
        in_specs=[pl.BlockSpec((CHUNK, conv_dim), lambda b, c: (b * nc + c, xbc_blk)),
                  pl.BlockSpec((CHUNK, hp), lambda b, c: (b * nc + c, z_blk)),
                  pl.BlockSpec((CHUNK, LANE), lambda b, c: (b * nc + c, dt_blk)),
                  pl.BlockSpec((1, CONV_W - 1, conv_dim), lambda b, c: (b, 0, 0)),
                  pl.BlockSpec((1, groups, D_STATE, gw), lambda b, c: (b, 0, 0, 0)),
                  pl.BlockSpec((CONV_W, conv_dim), const2),
                  pl.BlockSpec((1, conv_dim), const2),
                  pl.BlockSpec((1, LANE), const2),
                  pl.BlockSpec((1, LANE), const2),
                  pl.BlockSpec((1, hp), const2),
                  pl.BlockSpec((1, hp), const2),
                  pl.BlockSpec((LANE, hp), const2)],
        out_specs=[pl.BlockSpec((CHUNK, hp), lambda b, c: (b * nc + c, 0)),
                   pl.BlockSpec((1, groups, D_STATE, gw), lambda b, c: (b, 0, 0, 0))],
        out_shape=[jax.ShapeDtypeStruct((m, hp), BF16),
                   jax.ShapeDtypeStruct((batch, groups, D_STATE, gw), F32)],
        scratch_shapes=[pltpu.VMEM((8 + CHUNK, conv_dim), F32),
                        pltpu.VMEM((groups, D_STATE, gw), F32)],
        compiler_params=_params("parallel", "arbitrary"),
        name="ssd_mixer",
    )(proj, proj, proj, conv_init, h_init_t, conv_w, conv_b, dtb, alog, dsk_e, gn, e_mat)


def _mem_kv_kernel(x_ref, g_ref, w_ref, gk_ref, k_ref, v_ref, *, heads):
    x = x_ref[...]
    xn = (x * _rms(x, x.shape[-1]) * g_ref[...]).astype(BF16)
    width = heads * XA_HEAD_DIM
    kv = jnp.dot(xn, w_ref[...], preferred_element_type=F32)
    for h in range(heads):
        kh = kv[:, h * XA_HEAD_DIM:(h + 1) * XA_HEAD_DIM]
        k_ref[:, h * XA_HEAD_DIM:(h + 1) * XA_HEAD_DIM] = kh * _rms(kh, XA_HEAD_DIM) * gk_ref[...]
    v_ref[...] = kv[:, width:]


def memory_kv(mem, g, w_kv, gk, heads):
    m, d = mem.shape
    width = heads * XA_HEAD_DIM
    tm = _tile(m, 256)
    return pl.pallas_call(
        functools.partial(_mem_kv_kernel, heads=heads),
        grid=(m // tm,),
        in_specs=[pl.BlockSpec((tm, d), lambda i: (i, 0)),
                  pl.BlockSpec((1, d), lambda i: (0, 0)),
                  pl.BlockSpec((d, 2 * width), lambda i: (0, 0)),
                  pl.BlockSpec((1, XA_HEAD_DIM), lambda i: (0, 0))],
        out_specs=[pl.BlockSpec((tm, width), lambda i: (i, 0)),
                   pl.BlockSpec((tm, width), lambda i: (i, 0))],
        out_shape=[jax.ShapeDtypeStruct((m, width), F32), jax.ShapeDtypeStruct((m, width), F32)],
        compiler_params=_params("parallel"),
        name="memory_kv",
    )(mem, g.reshape(1, d), w_kv, gk.reshape(1, -1))


def _xattn_kernel(x_ref, gxa_ref, wq_ref, gq_ref, mk_ref, mv_ref, wo_ref, gffn_ref, xo_ref, ho_ref,
                  *, heads):
    x = x_ref[...]
    d = x.shape[-1]
    xn = (x * _rms(x, d) * gxa_ref[...]).astype(BF16)
    q = jnp.dot(xn, wq_ref[...], preferred_element_type=F32)
    outs = []
    for h in range(heads):
        cols = slice(h * XA_HEAD_DIM, (h + 1) * XA_HEAD_DIM)
        qh = q[:, cols]
        qh = (qh * _rms(qh, XA_HEAD_DIM) * gq_ref[...] * XA_SCALE).astype(BF16)
        kh = mk_ref[0, :, cols].astype(BF16)
        vh = mv_ref[0, :, cols].astype(BF16)
        s = lax.dot_general(qh, kh, (((1,), (1,)), ((), ())), preferred_element_type=F32)
        p = jnp.exp(s - jnp.max(s, axis=-1, keepdims=True))
        p = p / jnp.sum(p, axis=-1, keepdims=True)
        outs.append(jnp.dot(p.astype(BF16), vh, preferred_element_type=F32))
    o = jnp.concatenate(outs, axis=-1).astype(BF16)
    x_new = x + jnp.dot(o, wo_ref[...], preferred_element_type=F32)
    xo_ref[...] = x_new
    ho_ref[...] = (x_new * _rms(x_new, d) * gffn_ref[...]).astype(ho_ref.dtype)


def cross_attention_block(x, batch, gxa, wq, gq, mem_k, mem_v, wo, gffn, heads):
    m, d = x.shape
    s = m // batch
    n_mem = mem_k.shape[1]
    width = heads * XA_HEAD_DIM
    tm = _tile(s, 128)
    per_b = s // tm
    return pl.pallas_call(
        functools.partial(_xattn_kernel, heads=heads),
        grid=(m // tm,),
        in_specs=[pl.BlockSpec((tm, d), lambda i: (i, 0)),
                  pl.BlockSpec((1, d), lambda i: (0, 0)),
                  pl.BlockSpec((d, width), lambda i: (0, 0)),
                  pl.BlockSpec((1, XA_HEAD_DIM), lambda i: (0, 0)),
                  pl.BlockSpec((1, n_mem, width), lambda i: (i // per_b, 0, 0)),
                  pl.BlockSpec((1, n_mem, width), lambda i: (i // per_b, 0, 0)),
                  pl.BlockSpec((width, d), lambda i: (0, 0)),
                  pl.BlockSpec((1, d), lambda i: (0, 0))],
        out_specs=[pl.BlockSpec((tm, d), lambda i: (i, 0)),
                   pl.BlockSpec((tm, d), lambda i: (i, 0))],
        out_shape=[jax.ShapeDtypeStruct((m, d), F32), jax.ShapeDtypeStruct((m, d), BF16)],
        compiler_params=_params("parallel"),
        name="cross_attention_block",
    )(x, gxa.reshape(1, d), wq, gq.reshape(1, -1), mem_k, mem_v, wo, gffn.reshape(1, d))


def _layout(q_lora, kv_lora, ssm_inner, conv_dim):
    off = {}
    off["cq"] = 0
    off["ckv"] = _round_up(q_lora, kv_lora)
    off["kpe"] = _round_up(off["ckv"] + kv_lora, LANE)
    off["dt"] = off["kpe"] + LANE
    off["z"] = _round_up(off["dt"] + LANE, ssm_inner)
    off["xbc"] = _round_up(off["z"] + ssm_inner, conv_dim)
    off["total"] = _round_up(off["xbc"] + conv_dim, 1024)
    return off


_SWAP_HALVES = np.concatenate([np.arange(ROPE_DIM // 2, ROPE_DIM), np.arange(ROPE_DIM // 2)])


def _prep_layer(l, w, dims):
    q_lora, kv_lora, heads, ssm_heads, hp, conv_dim = (dims[k] for k in
                                                       ("q_lora", "kv_lora", "heads", "ssm_heads", "hp", "conv_dim"))
    off = dims["off"]
    w_in = w["w_in"][l]
    zero_col = w_in.shape[1]
    idx = np.full((off["total"],), zero_col, np.int32)
    o_ckv, o_kpe = q_lora, q_lora + kv_lora
    o_z = o_kpe + ROPE_DIM
    o_xbc = o_z + hp
    o_dt = o_xbc + conv_dim
    idx[off["cq"]:off["cq"] + q_lora] = np.arange(q_lora)
    idx[off["ckv"]:off["ckv"] + kv_lora] = o_ckv + np.arange(kv_lora)
    idx[off["kpe"]:off["kpe"] + ROPE_DIM] = o_kpe + np.arange(ROPE_DIM)
    idx[off["kpe"] + ROPE_DIM:off["kpe"] + 2 * ROPE_DIM] = o_kpe + _SWAP_HALVES
    idx[off["dt"]:off["dt"] + ssm_heads] = o_dt + np.arange(ssm_heads)
    idx[off["z"]:off["z"] + hp] = o_z + np.arange(hp)
    idx[off["xbc"]:off["xbc"] + conv_dim] = o_xbc + np.arange(conv_dim)
    w_in_ext = jnp.concatenate([w_in, jnp.zeros((w_in.shape[0], 1), w_in.dtype)], axis=1)
    p = {"w_in": jnp.take(w_in_ext, jnp.asarray(idx), axis=1).astype(BF16)}

    qk = NOPE_DIM + ROPE_DIM
    uq_idx = np.concatenate([np.concatenate([h * qk + np.arange(qk), h * qk + NOPE_DIM + _SWAP_HALVES])
                             for h in range(heads)])
    p["w_uq"] = jnp.take(w["w_uq"][l], jnp.asarray(uq_idx, np.int32), axis=1).astype(BF16)
    p["w_ukv"] = w["w_ukv"][l].astype(BF16)
    p["w_o"] = w["w_o"][l].astype(BF16)
    p["w_xq"] = w["w_xq"][l].astype(BF16)
    p["w_xkv"] = jnp.concatenate([w["w_xk"][l], w["w_xv"][l]], axis=1).astype(BF16)
    p["w_xo"] = w["w_xo"][l].astype(BF16)
    f = w["w_gate"].shape[-1]
    fp = _round_up(f, 512)
    p["w_gate"] = jnp.pad(w["w_gate"][l].astype(BF16), ((0, 0), (0, fp - f)))
    p["w_up"] = jnp.pad(w["w_up"][l].astype(BF16), ((0, 0), (0, fp - f)))
    p["w_down"] = jnp.pad(w["w_down"][l].astype(BF16), ((0, fp - f), (0, 0)))

    pad = LANE - ssm_heads
    p["dt_bias"] = jnp.pad(w["dt_bias"][l], (0, pad)).reshape(1, LANE)
    p["a_log"] = jnp.pad(w["a_log"][l], (0, pad)).reshape(1, LANE)
    p["d_skip_e"] = jnp.repeat(w["d_skip"][l], SSM_HEADDIM).reshape(1, hp)
    return p


def _rope_table(pos, gain):
    half = ROPE_DIM // 2
    inv = ROPE_BASE ** (-jnp.arange(half, dtype=F32) / half)
    ang = pos.astype(F32)[:, None] * inv[None, :]
    cos, sin = jnp.cos(ang), jnp.sin(ang)
    g1, g2 = gain[:half][None, :], gain[half:][None, :]
    return jnp.concatenate([g1 * cos, g2 * cos, -g2 * sin, g1 * sin], axis=-1)


def _block(x, batch, pos, past_ckv, past_kpe, conv_init, h_init_t, mem_k, mem_v, l, w, p, dims):
    m, d = x.shape
    s = m // batch
    heads, q_lora, kv_lora, hp, conv_dim = (dims[k] for k in ("heads", "q_lora", "kv_lora", "hp", "conv_dim"))
    off = dims["off"]

    h = rmsnorm_bf16(x, w["norm_mix"][l])
    proj = matmul(h, p["w_in"], out_dtype=F32)

    tab_q = _rope_table(pos, w["qn_pe"][l])
    tab_k = _rope_table(pos, w["kn_pe"][l])
    q = q_proj(proj, off["cq"] // q_lora, q_lora, w["q_norm"][l], p["w_uq"], w["qn_nope"][l], tab_q, heads)
    ckv_new, kpe_new = kv_norm(proj, off["ckv"] // kv_lora, kv_lora, off["kpe"] // LANE,
                               w["kv_norm"][l], tab_k)
    if past_ckv is None:
        ckv_all, kpe_all = ckv_new, kpe_new
    else:
        ckv_all = jnp.concatenate([past_ckv, ckv_new.reshape(batch, s, kv_lora)], axis=1).reshape(-1, kv_lora)
        kpe_all = jnp.concatenate([past_kpe, kpe_new.reshape(batch, s, ROPE_DIM)], axis=1).reshape(-1, ROPE_DIM)
    k, v = kv_up(ckv_all, kpe_all, p["w_ukv"], w["kn_nope"][l], heads)
    attn = flash_attention(q, k, v, batch)

    y, h_new_t = ssd_mixer(proj, batch, off["xbc"] // conv_dim, conv_dim, off["z"] // hp, hp,
                           off["dt"] // LANE, conv_init, h_init_t, w["conv_w"][l],
                           w["conv_b"][l].reshape(1, -1), p["dt_bias"], p["a_log"], p["d_skip_e"],
                           w["ssm_norm"][l].reshape(1, -1), dims["e_mat"], dims["groups"], dims["rep"])
    conv_new = proj.reshape(batch, s, -1)[:, s - (CONV_W - 1):, off["xbc"]:off["xbc"] + conv_dim]

    x = matmul(jnp.concatenate([attn, y], axis=-1), p["w_o"], out_dtype=F32, residual=x)
    x, h_ffn = cross_attention_block(x, batch, w["norm_xa"][l], p["w_xq"], w["xa_norm_q"][l],
                                     mem_k, mem_v, p["w_xo"], w["norm_ffn"][l], dims["xa_heads"])
    hid = swiglu_up(h_ffn, p["w_gate"], p["w_up"])
    x = matmul(hid, p["w_down"], out_dtype=F32, residual=x, tm=512, tn=1024, tk=hid.shape[1] // 2)
    return x, ckv_new, kpe_new, conv_new, h_new_t


def _state_to_t(h, groups, rep):
    b = h.shape[0]
    return h.reshape(b, groups, rep * SSM_HEADDIM, D_STATE).transpose(0, 1, 3, 2)


def _state_from_t(ht, groups, rep):
    b = ht.shape[0]
    return ht.transpose(0, 1, 3, 2).reshape(b, groups * rep, SSM_HEADDIM, D_STATE)


def kernel(x_prompt, x_sample, mem_prompt, cache_ckv, cache_kpe, state_conv, state_ssm, cache_mem_k, cache_mem_v, norm_mix, w_in, q_norm, w_uq, kv_norm, w_ukv, qn_nope, qn_pe, kn_nope, kn_pe, conv_w, conv_b, dt_bias, a_log, d_skip, ssm_norm, w_o, norm_xa, mem_norm, w_xq, w_xk, w_xv, xa_norm_q, xa_norm_k, w_xo, norm_ffn, w_gate, w_up, w_down):
    w = dict(norm_mix=norm_mix, w_in=w_in, q_norm=q_norm, w_uq=w_uq, kv_norm=kv_norm, w_ukv=w_ukv,
             qn_nope=qn_nope, qn_pe=qn_pe, kn_nope=kn_nope, kn_pe=kn_pe, conv_w=conv_w, conv_b=conv_b,
             dt_bias=dt_bias, a_log=a_log, d_skip=d_skip, ssm_norm=ssm_norm, w_o=w_o, norm_xa=norm_xa,
             mem_norm=mem_norm, w_xq=w_xq, w_xk=w_xk, w_xv=w_xv, xa_norm_q=xa_norm_q,
             xa_norm_k=xa_norm_k, w_xo=w_xo, norm_ffn=norm_ffn, w_gate=w_gate, w_up=w_up, w_down=w_down)
    bp, sp, d = x_prompt.shape
    bs, ss, _ = x_sample.shape
    depth = w_in.shape[0]
    past = cache_ckv.shape[2]
    q_lora, kv_lora = q_norm.shape[-1], kv_norm.shape[-1]
    heads = w_uq.shape[-1] // (NOPE_DIM + ROPE_DIM)
    ssm_heads, hp, conv_dim = dt_bias.shape[-1], ssm_norm.shape[-1], conv_w.shape[-1]
    groups = (conv_dim - hp) // (2 * D_STATE)
    rep = ssm_heads // groups
    xa_heads = w_xq.shape[-1] // XA_HEAD_DIM
    n_mem = mem_prompt.shape[1]
    assert ssm_heads <= LANE and hp == ssm_heads * SSM_HEADDIM and sp % CHUNK == 0 and ss % CHUNK == 0

    e_mat = (np.arange(LANE)[:, None] == (np.arange(hp) // SSM_HEADDIM)[None, :]).astype(np.float32)
    dims = dict(q_lora=q_lora, kv_lora=kv_lora, heads=heads, ssm_heads=ssm_heads, hp=hp,
                conv_dim=conv_dim, groups=groups, rep=rep, xa_heads=xa_heads,
                off=_layout(q_lora, kv_lora, hp, conv_dim), e_mat=jnp.asarray(e_mat))

    pos_p = jnp.arange(sp, dtype=jnp.int32)
    pos_s = past + jnp.arange(ss, dtype=jnp.int32)
    conv0 = jnp.zeros((bp, CONV_W - 1, conv_dim), F32)
    ssm0_t = jnp.zeros((bp, groups, D_STATE, rep * SSM_HEADDIM), F32)

    hp_x = x_prompt.reshape(bp * sp, d)
    hs_x = x_sample.reshape(bs * ss, d)
    mem2d = mem_prompt.reshape(bp * n_mem, d)
    outs = {k: [] for k in ("ckv_p", "kpe_p", "conv_p", "ssm_p", "mk_p", "mv_p",
                            "ckv_s", "kpe_s", "conv_s", "ssm_s")}
    for l in range(depth):
        p = _prep_layer(l, w, dims)
        mk, mv = memory_kv(mem2d, mem_norm[l], p["w_xkv"], xa_norm_k[l], xa_heads)
        mk = mk.reshape(bp, n_mem, -1)
        mv = mv.reshape(bp, n_mem, -1)
        hp_x, a, b, c, dd = _block(hp_x, bp, pos_p, None, None, conv0, ssm0_t, mk, mv, l, w, p, dims)
        outs["ckv_p"].append(a.reshape(bp, sp, kv_lora))
        outs["kpe_p"].append(b.reshape(bp, sp, ROPE_DIM))
        outs["conv_p"].append(c)
        outs["ssm_p"].append(_state_from_t(dd, groups, rep))
        outs["mk_p"].append(mk.reshape(bp, n_mem, xa_heads, XA_HEAD_DIM))
        outs["mv_p"].append(mv.reshape(bp, n_mem, xa_heads, XA_HEAD_DIM))
        hs_x, a, b, c, dd = _block(hs_x, bs, pos_s, cache_ckv[l], cache_kpe[l], state_conv[l],
                                   _state_to_t(state_ssm[l], groups, rep),
                                   cache_mem_k[l].reshape(bs, n_mem, -1),
                                   cache_mem_v[l].reshape(bs, n_mem, -1), l, w, p, dims)
        outs["ckv_s"].append(a.reshape(bs, ss, kv_lora))
        outs["kpe_s"].append(b.reshape(bs, ss, ROPE_DIM))
        outs["conv_s"].append(c)
        outs["ssm_s"].append(_state_from_t(dd, groups, rep))

    st = lambda k: jnp.stack(outs[k])
    return (hp_x.reshape(bp, sp, d), hs_x.reshape(bs, ss, d),
            st("ckv_p"), st("kpe_p"), st("conv_p"), st("ssm_p"), st("mk_p"), st("mv_p"),
            st("ckv_s"), st("kpe_s"), st("conv_s"), st("ssm_s"))
```

```python
import functools

import numpy as np
import jax
import jax.numpy as jnp
from jax import lax
from jax.experimental import pallas as pl
from jax.experimental.pallas import tpu as pltpu

F32 = jnp.float32
BF16 = jnp.bfloat16
HIGHEST = lax.Precision.HIGHEST

EPS = 1e-6
CHUNK = 64
CHUNK_SHIFT = 6
NOPE_DIM = 128
ROPE_DIM = 64
V_DIM = 128
QK_SLAB = 256
V_SLAB = 256
ROPE_BASE = 10000.0
MLA_SCALE = (NOPE_DIM + ROPE_DIM) ** -0.5
SSM_HEADDIM = 64
D_STATE = 128
CONV_W = 4
XA_HEAD_DIM = 128
XA_SCALE = XA_HEAD_DIM ** -0.5
LANE = 128
VMEM_LIMIT = 56 * 1024 * 1024


def _round_up(n, m):
    return -(-n // m) * m


def _tile(n, pref, mult=8):
    if n <= pref:
        return n
    t = pref - pref % mult
    while t >= mult:
        if n % t == 0:
            return t
        t -= mult
    return n


def _params(*sem):
    return pltpu.CompilerParams(dimension_semantics=sem, vmem_limit_bytes=VMEM_LIMIT)


def _resident(shape, index_map):
    return pl.BlockSpec(shape, index_map, pipeline_mode=pl.Buffered(1))


def _rms(x, width):
    return lax.rsqrt(jnp.sum(x * x, axis=-1, keepdims=True) * (1.0 / width) + EPS)


def _silu(x):
    return x / (1.0 + jnp.exp(-x))


def _rmsnorm_kernel(x_ref, g_ref, o_ref):
    x = x_ref[...]
    o_ref[...] = (x * _rms(x, x.shape[-1]) * g_ref[...]).astype(o_ref.dtype)


def rmsnorm_bf16(x, g):
    m, d = x.shape
    tm = _tile(m, 256)
    return pl.pallas_call(
        _rmsnorm_kernel,
        grid=(m // tm,),
        in_specs=[pl.BlockSpec((tm, d), lambda i: (i, 0)), pl.BlockSpec((1, d), lambda i: (0, 0))],
        out_specs=pl.BlockSpec((tm, d), lambda i: (i, 0)),
        out_shape=jax.ShapeDtypeStruct((m, d), BF16),
        compiler_params=_params("parallel"),
        name="rmsnorm_bf16",
    )(x, g.reshape(1, d))


def _mm_kernel(*refs, nk, has_res):
    if has_res:
        a_ref, b_ref, r_ref, o_ref = refs[:4]
    else:
        a_ref, b_ref, o_ref = refs[:3]
        r_ref = None

    def part():
        return jnp.dot(a_ref[...], b_ref[...], preferred_element_type=F32)

    def finish(total):
        if has_res:
            total = total + r_ref[...]
        o_ref[...] = total.astype(o_ref.dtype)

    if nk == 1:
        finish(part())
        return
    acc_ref = refs[-1]
    k = pl.program_id(2)

    @pl.when(k == 0)
    def _():
        acc_ref[...] = part()

    if nk > 2:
        @pl.when(jnp.logical_and(k > 0, k < nk - 1))
        def _():
            acc_ref[...] += part()

    @pl.when(k == nk - 1)
    def _():
        finish(acc_ref[...] + part())


def matmul(a, b, layer, *, out_dtype, residual=None, tm=1024, tn=1024, tk=None):
    m, kd = a.shape
    n = b.shape[-1]
    tm = _tile(m, tm)
    tn = _tile(n, tn, LANE)
    tk = kd if tk is None else _tile(kd, tk, LANE)
    nk = kd // tk
    in_specs = [pl.BlockSpec((tm, tk), lambda i, j, k: (i, k)),
                pl.BlockSpec((None, tk, tn), lambda i, j, k: (layer, k, j))]
    args = [a, b]
    if residual is not None:
        in_specs.append(pl.BlockSpec((tm, tn), lambda i, j, k: (i, j)))
        args.append(residual)
    scratch = [pltpu.VMEM((tm, tn), F32)] if nk > 1 else []
    return pl.pallas_call(
        functools.partial(_mm_kernel, nk=nk, has_res=residual is not None),
        grid=(m // tm, n // tn, nk),
        in_specs=in_specs,
        out_specs=pl.BlockSpec((tm, tn), lambda i, j, k: (i, j)),
        out_shape=jax.ShapeDtypeStruct((m, n), out_dtype),
        scratch_shapes=scratch,
        compiler_params=_params("parallel", "parallel", "arbitrary"),
        name="matmul",
    )(*args)


def _mm2_kernel(a1_ref, a2_ref, b_ref, r_ref, o_ref):
    k1 = a1_ref.shape[1]
    acc = jnp.dot(a1_ref[...], b_ref[:k1, :], preferred_element_type=F32)
    acc = acc + jnp.dot(a2_ref[...], b_ref[k1:, :], preferred_element_type=F32)
    o_ref[...] = acc + r_ref[...]


def matmul_concat_residual(a1, a2, b, layer, residual, *, tm=1024, tn=512):
    m, k1 = a1.shape
    k2 = a2.shape[1]
    n = b.shape[-1]
    tm = _tile(m, tm)
    tn = _tile(n, tn, LANE)
    return pl.pallas_call(
        _mm2_kernel,
        grid=(m // tm, n // tn),
        in_specs=[pl.BlockSpec((tm, k1), lambda i, j: (i, 0)),
                  pl.BlockSpec((tm, k2), lambda i, j: (i, 0)),
                  pl.BlockSpec((None, k1 + k2, tn), lambda i, j: (layer, 0, j)),
                  pl.BlockSpec((tm, tn), lambda i, j: (i, j))],
        out_specs=pl.BlockSpec((tm, tn), lambda i, j: (i, j)),
        out_shape=jax.ShapeDtypeStruct((m, n), F32),
        compiler_params=_params("parallel", "parallel"),
        name="matmul_concat_residual",
    )(a1, a2, b, residual)


def _swiglu_kernel(a_ref, wg_ref, wu_ref, o_ref):
    a = a_ref[...]
    g = jnp.dot(a, wg_ref[...].astype(BF16), preferred_element_type=F32)
    u = jnp.dot(a, wu_ref[...].astype(BF16), preferred_element_type=F32)
    o_ref[...] = (_silu(g) * u).astype(o_ref.dtype)


def swiglu_up(a, wg, wu, layer, *, tm=2048, tn=256):
    m, d = a.shape
    f = wg.shape[-1]
    tm = _tile(m, tm)
    tn = _tile(f, tn, LANE)
    w_spec = pl.BlockSpec((None, d, tn), lambda i, j: (layer, 0, j))
    return pl.pallas_call(
        _swiglu_kernel,
        grid=(m // tm, f // tn),
        in_specs=[_resident((tm, d), lambda i, j: (i, 0)), w_spec, w_spec],
        out_specs=pl.BlockSpec((tm, tn), lambda i, j: (i, j)),
        out_shape=jax.ShapeDtypeStruct((m, f), BF16),
        compiler_params=_params("parallel", "parallel"),
        name="swiglu_up",
    )(a, wg, wu)


def _q_kernel(cq_ref, gq_ref, w_ref, gn_ref, tab_ref, q_ref, *, heads):
    x = cq_ref[...]
    xn = (x * _rms(x, x.shape[-1]) * gq_ref[...]).astype(BF16)
    tab = tab_ref[...]
    gn = gn_ref[...]
    for h in range(heads):
        acc = jnp.dot(xn, w_ref[:, h * QK_SLAB:(h + 1) * QK_SLAB], preferred_element_type=F32)
        nope = acc[:, :NOPE_DIM]
        rot = acc[:, NOPE_DIM:]
        qn = nope * _rms(nope, NOPE_DIM) * gn
        qr = rot * _rms(rot, 2 * ROPE_DIM) * tab
        q_ref[h, :, :NOPE_DIM] = (qn * MLA_SCALE).astype(BF16)
        q_ref[h, :, NOPE_DIM:] = (qr * MLA_SCALE).astype(BF16)


def q_proj(proj, col_blk, q_lora, gq, w_uq_r, layer, gn, tab, heads):
    m = proj.shape[0]
    s_tab = tab.shape[0]
    tm = _tile(s_tab, 256)
    nt = s_tab // tm
    return pl.pallas_call(
        functools.partial(_q_kernel, heads=heads),
        grid=(m // tm,),
        in_specs=[pl.BlockSpec((tm, q_lora), lambda i: (i, col_blk)),
                  pl.BlockSpec((1, q_lora), lambda i: (0, 0)),
                  _resident((None, q_lora, heads * QK_SLAB), lambda i: (layer, 0, 0)),
                  pl.BlockSpec((1, NOPE_DIM), lambda i: (0, 0)),
                  pl.BlockSpec((tm, 2 * ROPE_DIM), lambda i: (i % nt, 0))],
        out_specs=pl.BlockSpec((heads, tm, QK_SLAB), lambda i: (0, i, 0)),
        out_shape=jax.ShapeDtypeStruct((heads, m, QK_SLAB), BF16),
        compiler_params=_params("parallel"),
        name="q_proj",
    )(proj, gq.reshape(1, -1), w_uq_r, gn.reshape(1, -1), tab)


def _kv_norm_kernel(ckv_ref, kpe_ref, g_ref, tab_ref, ckv_o, kpe_o):
    x = ckv_ref[...]
    ckv_o[...] = x * _rms(x, x.shape[-1]) * g_ref[...]
    t = kpe_ref[...]
    u = t * _rms(t, 2 * ROPE_DIM) * tab_ref[...]
    kpe_o[...] = u[:, :ROPE_DIM] + u[:, ROPE_DIM:]


def kv_norm(proj, ckv_blk, kv_lora, kpe_blk, g, tab):
    m = proj.shape[0]
    s_tab = tab.shape[0]
    tm = _tile(s_tab, 512)
    nt = s_tab // tm
    return pl.pallas_call(
        _kv_norm_kernel,
        grid=(m // tm,),
        in_specs=[pl.BlockSpec((tm, kv_lora), lambda i: (i, ckv_blk)),
                  pl.BlockSpec((tm, 2 * ROPE_DIM), lambda i: (i, kpe_blk)),
                  pl.BlockSpec((1, kv_lora), lambda i: (0, 0)),
                  pl.BlockSpec((tm, 2 * ROPE_DIM), lambda i: (i % nt, 0))],
        out_specs=[pl.BlockSpec((tm, kv_lora), lambda i: (i, 0)),
                   pl.BlockSpec((tm, ROPE_DIM), lambda i: (i, 0))],
        out_shape=[jax.ShapeDtypeStruct((m, kv_lora), F32),
                   jax.ShapeDtypeStruct((m, ROPE_DIM), F32)],
        compiler_params=_params("parallel"),
        name="kv_norm",
    )(proj, proj, g.reshape(1, -1), tab)


def _kv_up_kernel(ckv_ref, kpe_ref, w_ref, gn_ref, k_ref, v_ref, *, heads):
    xb = ckv_ref[...].astype(BF16)
    kp = kpe_ref[...]
    kp2 = jnp.concatenate([kp, kp], axis=-1).astype(BF16)
    gn = gn_ref[...]
    ones = jnp.ones((xb.shape[0], V_SLAB - V_DIM), BF16)
    for h in range(heads):
        acc = jnp.dot(xb, w_ref[:, h * 256:(h + 1) * 256], preferred_element_type=F32)
        nope = acc[:, :NOPE_DIM]
        k_ref[h, :, :NOPE_DIM] = (nope * _rms(nope, NOPE_DIM) * gn).astype(BF16)
        k_ref[h, :, NOPE_DIM:] = kp2
        v_ref[h, :, :V_DIM] = acc[:, NOPE_DIM:].astype(BF16)
        v_ref[h, :, V_DIM:] = ones


def kv_up(ckv, kpe, w_ukv, layer, gn, heads):
    m, kv_lora = ckv.shape
    tm = _tile(m, 256)
    return pl.pallas_call(
        functools.partial(_kv_up_kernel, heads=heads),
        grid=(m // tm,),
        in_specs=[pl.BlockSpec((tm, kv_lora), lambda i: (i, 0)),
                  pl.BlockSpec((tm, ROPE_DIM), lambda i: (i, 0)),
                  _resident((None, kv_lora, heads * 256), lambda i: (layer, 0, 0)),
                  pl.BlockSpec((1, NOPE_DIM), lambda i: (0, 0))],
        out_specs=[pl.BlockSpec((heads, tm, QK_SLAB), lambda i: (0, i, 0)),
                   pl.BlockSpec((heads, tm, V_SLAB), lambda i: (0, i, 0))],
        out_shape=[jax.ShapeDtypeStruct((heads, m, QK_SLAB), BF16),
                   jax.ShapeDtypeStruct((heads, m, V_SLAB), BF16)],
        compiler_params=_params("parallel"),
        name="kv_up",
    )(ckv, kpe, w_ukv, gn.reshape(1, -1))


def _flash_kernel(q_ref, k_ref, v_ref, o_ref, m_ref, acc_ref, *, heads, tq, tk, nk, q_off, unroll):
    i = pl.program_id(1)
    j = pl.program_id(2)
    q_lo = q_off + i * tq
    j_last = jnp.minimum((((q_lo + tq - 1) // CHUNK + 1) * CHUNK - 1) // tk, nk - 1)
    fully_visible = ((j * tk + tk - 1) // CHUNK) <= (q_lo // CHUNK)

    @pl.when(j == 0)
    def _():
        m_ref[...] = jnp.full(m_ref.shape, -jnp.inf, F32)
        acc_ref[...] = jnp.zeros(acc_ref.shape, F32)

    def tile(masked):
        if masked:
            q_chunk = (q_lo + lax.broadcasted_iota(jnp.int32, (tq, tk), 0)) >> CHUNK_SHIFT
            k_chunk = (j * tk + lax.broadcasted_iota(jnp.int32, (tq, tk), 1)) >> CHUNK_SHIFT
            visible = k_chunk <= q_chunk

        def head(h, carry):
            s = lax.dot_general(q_ref[h], k_ref[h], (((1,), (1,)), ((), ())),
                                preferred_element_type=F32)
            if masked:
                s = jnp.where(visible, s, -jnp.inf)
            m_prev = m_ref[h]
            m_new = jnp.maximum(m_prev, jnp.max(s, axis=-1, keepdims=True))
            alpha = jnp.exp(m_prev - m_new)
            m_b = pltpu.repeat(m_new, tk // LANE, axis=1) if tk % LANE == 0 else m_new[:, :1]
            p = jnp.exp(s - m_b)
            pv = jnp.dot(p.astype(BF16), v_ref[h], preferred_element_type=F32)
            acc_ref[h] = pltpu.repeat(alpha, V_SLAB // LANE, axis=1) * acc_ref[h] + pv
            m_ref[h] = m_new
            return carry

        lax.fori_loop(0, heads, head, 0, unroll=unroll)

    @pl.when(jnp.logical_and(j <= j_last, fully_visible))
    def _():
        tile(False)

    @pl.when(jnp.logical_and(j <= j_last, jnp.logical_not(fully_visible)))
    def _():
        tile(True)

    @pl.when(j == nk - 1)
    def _():
        for h in range(heads):
            acc = acc_ref[h]
            o_ref[:, h * V_DIM:(h + 1) * V_DIM] = (acc[:, :V_DIM] / acc[:, V_DIM:]).astype(o_ref.dtype)


def flash_attention(q, k, v, batch, *, tq=512, tk=512, unroll=8):
    heads, mq, _ = q.shape
    mk = k.shape[1]
    sq, sk = mq // batch, mk // batch
    tq = _tile(sq, tq)
    tk = _tile(sk, tk, 16)
    nq, nk = sq // tq, sk // tk
    q_off = sk - sq
    unroll = min(unroll, heads)

    def kv_index(b, i, j):
        j_last = (((q_off + i * tq + tq - 1) // CHUNK + 1) * CHUNK - 1) // tk
        return (0, b * nk + jnp.minimum(j, jnp.minimum(j_last, nk - 1)), 0)

    return pl.pallas_call(
        functools.partial(_flash_kernel, heads=heads, tq=tq, tk=tk, nk=nk, q_off=q_off, unroll=unroll),
        grid=(batch, nq, nk),
        in_specs=[pl.BlockSpec((heads, tq, QK_SLAB), lambda b, i, j: (0, b * nq + i, 0)),
                  pl.BlockSpec((heads, tk, QK_SLAB), kv_index),
                  pl.BlockSpec((heads, tk, V_SLAB), kv_index)],
        out_specs=pl.BlockSpec((tq, heads * V_DIM), lambda b, i, j: (b * nq + i, 0)),
        out_shape=jax.ShapeDtypeStruct((mq, heads * V_DIM), BF16),
        scratch_shapes=[pltpu.VMEM((heads, tq, LANE), F32),
                        pltpu.VMEM((heads, tq, V_SLAB), F32)],
        compiler_params=_params("parallel", "parallel", "arbitrary"),
        name="flash_attention",
    )(q, k, v)


def _ssd_kernel(xbc_ref, z_ref, dt_ref, cinit_ref, hinit_ref, cw_ref, cb_ref, dtb_ref, alog_ref,
                dsk_ref, gn_ref, e_ref, y_ref, hout_ref, buf, ht, *, groups, rep, nc):
    c = pl.program_id(1)
    q = CHUNK
    gw = rep * SSM_HEADDIM
    hp = groups * gw
    halo = CONV_W - 1
    base = 8

    @pl.when(c == 0)
    def _():
        buf[base - halo:base, :] = cinit_ref[0]
        ht[...] = hinit_ref[0]

    buf[base:base + q, :] = xbc_ref[...]
    conv = cb_ref[...]
    for i in range(CONV_W):
        conv = conv + buf[base - halo + i:base - halo + i + q, :] * cw_ref[i:i + 1, :]
    tail = buf[base + q - halo:base + q, :]
    buf[base - halo:base, :] = tail
    xc = _silu(conv)
    xs = xc[:, :hp]
    bm = xc[:, hp:hp + groups * D_STATE].astype(BF16)
    cm = xc[:, hp + groups * D_STATE:].astype(BF16)

    dt_in = dt_ref[...] + dtb_ref[...]
    dt = jnp.maximum(dt_in, 0.0) + jnp.log1p(jnp.exp(-jnp.abs(dt_in)))
    a = dt * (-jnp.exp(alog_ref[...]))
    row = lax.broadcasted_iota(jnp.int32, (q, q), 0)
    col = lax.broadcasted_iota(jnp.int32, (q, q), 1)
    tri = (col <= row).astype(F32)
    a_cum = jnp.dot(tri, a, precision=HIGHEST, preferred_element_type=F32)
    a_col = jnp.dot(a_cum, e_ref[...], precision=HIGHEST, preferred_element_type=F32)
    dt_col = jnp.dot(dt, e_ref[...], precision=HIGHEST, preferred_element_type=F32)

    r_idx = lax.broadcasted_iota(jnp.int32, (q, hp), 0)
    l_idx = lax.broadcasted_iota(jnp.int32, (q, hp), 1) & (q - 1)
    a_row = jnp.sum(jnp.where(l_idx == r_idx, a_col, 0.0), axis=0, keepdims=True)
    decay = jnp.exp(jnp.where(l_idx <= r_idx, a_col - a_row, -jnp.inf))
    a_last = a_col[q - 1:q, :]
    exp_a = jnp.exp(a_col)
    decay_end = jnp.exp(a_last - a_col)
    block_decay = jnp.exp(a_last)

    xdt = xs * dt_col
    xw = (xdt * decay_end).astype(BF16)
    y_skip = dsk_ref[...] * xs
    zg = _silu(z_ref[...])

    rr = lax.broadcasted_iota(jnp.int32, (rep * q, gw), 0) >> CHUNK_SHIFT
    cc = lax.broadcasted_iota(jnp.int32, (rep * q, gw), 1) >> CHUNK_SHIFT
    diag_blocks = rr == cc

    for g in range(groups):
        lanes = slice(g * gw, (g + 1) * gw)
        bg = bm[:, g * D_STATE:(g + 1) * D_STATE]
        cg = cm[:, g * D_STATE:(g + 1) * D_STATE]
        b_rep = jnp.concatenate([bg] * rep, axis=0)
        cb = lax.dot_general(cg, b_rep, (((1,), (1,)), ((), ())), preferred_element_type=F32)
        mg = (cb * decay[:, lanes]).astype(BF16)
        xdt_g = xdt[:, lanes]
        x_blk = jnp.where(diag_blocks, jnp.concatenate([xdt_g] * rep, axis=0), 0.0).astype(BF16)
        y_diag = jnp.dot(mg, x_blk, preferred_element_type=F32)
        h_in = ht[g]
        y_off = jnp.dot(cg, h_in.astype(BF16), preferred_element_type=F32) * exp_a[:, lanes]
        st = lax.dot_general(bg, xw[:, lanes], (((0,), (0,)), ((), ())),
                             preferred_element_type=F32)
        ht[g] = h_in * block_decay[:, lanes] + st
        yg = (y_diag + y_off + y_skip[:, lanes]) * zg[:, lanes]
        y_ref[:, lanes] = (yg * _rms(yg, gw) * gn_ref[:, lanes]).astype(y_ref.dtype)

    @pl.when(c == nc - 1)
    def _():
        hout_ref[0] = ht[...]


def ssd_mixer(proj, batch, xbc_blk, conv_dim, z_blk, hp, dt_blk, conv_init, h_init_t, conv_w,
              conv_b, dtb, alog, dsk_e, gn, e_mat, groups, rep):
    m = proj.shape[0]
    nc = m // batch // CHUNK
    gw = rep * SSM_HEADDIM
    const2 = lambda b, c: (0, 0)
    return pl.pallas_call(
        functools.partial(_ssd_kernel, groups=groups, rep=rep, nc=nc),
        grid=(batch, nc),
        in_specs=[pl.BlockSpec((CHUNK, conv_dim), lambda b, c: (b * nc + c, xbc_blk)),
                  pl.BlockSpec((CHUNK, hp), lambda b, c: (b * nc + c, z_blk)),
                  pl.BlockSpec((CHUNK, LANE), lambda b, c: (b * nc + c, dt_blk)),
                  pl.BlockSpec((1, CONV_W - 1, conv_dim), lambda b, c: (b, 0, 0)),
                  pl.BlockSpec((1, groups, D_STATE, gw), lambda b, c: (b, 0, 0, 0)),
                  pl.BlockSpec((CONV_W, conv_dim), const2),
                  pl.BlockSpec((1, conv_dim), const2),
                  pl.BlockSpec((1, LANE), const2),
                  pl.BlockSpec((1, LANE), const2),
                  pl.BlockSpec((1, hp), const2),
                  pl.BlockSpec((1, hp), const2),
                  pl.BlockSpec((LANE, hp), const2)],
        out_specs=[pl.BlockSpec((CHUNK, hp), lambda b, c: (b * nc + c, 0)),
                   pl.BlockSpec((1, groups, D_STATE, gw), lambda b, c: (b, 0, 0, 0))],
        out_shape=[jax.ShapeDtypeStruct((m, hp), BF16),
                   jax.ShapeDtypeStruct((batch, groups, D_STATE, gw), F32)],
        scratch_shapes=[pltpu.VMEM((8 + CHUNK, conv_dim), F32),
                        pltpu.VMEM((groups, D_STATE, gw), F32)],
        compiler_params=_params("parallel", "arbitrary"),
        name="ssd_mixer",
    )(proj, proj, proj, conv_init, h_init_t, conv_w, conv_b, dtb, alog, dsk_e, gn, e_mat)


def _mem_kv_kernel(x_ref, g_ref, w_ref, gk_ref, k_ref, v_ref, *, heads):
    x = x_ref[...]
    xn = (x * _rms(x, x.shape[-1]) * g_ref[...]).astype(BF16)
    width = heads * XA_HEAD_DIM
    kv = jnp.dot(xn, w_ref[...], preferred_element_type=F32)
    for h in range(heads):
        kh = kv[:, h * XA_HEAD_DIM:(h + 1) * XA_HEAD_DIM]
        k_ref[:, h * XA_HEAD_DIM:(h + 1) * XA_HEAD_DIM] = kh * _rms(kh, XA_HEAD_DIM) * gk_ref[...]
    v_ref[...] = kv[:, width:]


def memory_kv(mem, g, w_kv, layer, gk, heads):
    m, d = mem.shape
    width = heads * XA_HEAD_DIM
    tm = _tile(m, 256)
    return pl.pallas_call(
        functools.partial(_mem_kv_kernel, heads=heads),
        grid=(m // tm,),
        in_specs=[pl.BlockSpec((tm, d), lambda i: (i, 0)),
                  pl.BlockSpec((1, d), lambda i: (0, 0)),
                  _resident((None, d, 2 * width), lambda i: (layer, 0, 0)),
                  pl.BlockSpec((1, XA_HEAD_DIM), lambda i: (0, 0))],
        out_specs=[pl.BlockSpec((tm, width), lambda i: (i, 0)),
                   pl.BlockSpec((tm, width), lambda i: (i, 0))],
        out_shape=[jax.ShapeDtypeStruct((m, width), F32), jax.ShapeDtypeStruct((m, width), F32)],
        compiler_params=_params("parallel"),
        name="memory_kv",
    )(mem, g.reshape(1, d), w_kv, gk.reshape(1, -1))


def _xattn_kernel(x_ref, gxa_ref, wq_ref, gq_ref, mk_ref, mv_ref, wo_ref, gffn_ref, xo_ref, ho_ref,
                  *, heads):
    x = x_ref[...]
    d = x.shape[-1]
    xn = (x * _rms(x, d) * gxa_ref[...]).astype(BF16)
    q = jnp.dot(xn, wq_ref[...], preferred_element_type=F32)
    outs = []
    for h in range(heads):
        cols = slice(h * XA_HEAD_DIM, (h + 1) * XA_HEAD_DIM)
        qh = q[:, cols]
        qh = (qh * _rms(qh, XA_HEAD_DIM) * gq_ref[...] * XA_SCALE).astype(BF16)
        kh = mk_ref[0, :, cols].astype(BF16)
        vh = mv_ref[0, :, cols].astype(BF16)
        s = lax.dot_general(qh, kh, (((1,), (1,)), ((), ())), preferred_element_type=F32)
        p = jnp.exp(s - jnp.max(s, axis=-1, keepdims=True))
        p = p / jnp.sum(p, axis=-1, keepdims=True)
        outs.append(jnp.dot(p.astype(BF16), vh, preferred_element_type=F32))
    o = jnp.concatenate(outs, axis=-1).astype(BF16)
    x_new = x + jnp.dot(o, wo_ref[...], preferred_element_type=F32)
    xo_ref[...] = x_new
    ho_ref[...] = (x_new * _rms(x_new, d) * gffn_ref[...]).astype(ho_ref.dtype)


def cross_attention_block(x, batch, gxa, wq, gq, mem_k, mem_v, wo, layer, gffn, heads):
    m, d = x.shape
    s = m // batch
    n_mem = mem_k.shape[1]
    width = heads * XA_HEAD_DIM
    tm = _tile(s, 256)
    per_b = s // tm
    return pl.pallas_call(
        functools.partial(_xattn_kernel, heads=heads),
        grid=(m // tm,),
        in_specs=[pl.BlockSpec((tm, d), lambda i: (i, 0)),
                  pl.BlockSpec((1, d), lambda i: (0, 0)),
                  _resident((None, d, width), lambda i: (layer, 0, 0)),
                  pl.BlockSpec((1, XA_HEAD_DIM), lambda i: (0, 0)),
                  pl.BlockSpec((1, n_mem, width), lambda i: (i // per_b, 0, 0)),
                  pl.BlockSpec((1, n_mem, width), lambda i: (i // per_b, 0, 0)),
                  _resident((None, width, d), lambda i: (layer, 0, 0)),
                  pl.BlockSpec((1, d), lambda i: (0, 0))],
        out_specs=[pl.BlockSpec((tm, d), lambda i: (i, 0)),
                   pl.BlockSpec((tm, d), lambda i: (i, 0))],
        out_shape=[jax.ShapeDtypeStruct((m, d), F32), jax.ShapeDtypeStruct((m, d), BF16)],
        compiler_params=_params("parallel"),
        name="cross_attention_block",
    )(x, gxa.reshape(1, d), wq, gq.reshape(1, -1), mem_k, mem_v, wo, gffn.reshape(1, d))


def _layout(q_lora, kv_lora, ssm_inner, conv_dim):
    off = {}
    off["cq"] = 0
    off["ckv"] = _round_up(q_lora, kv_lora)
    off["kpe"] = _round_up(off["ckv"] + kv_lora, LANE)
    off["dt"] = off["kpe"] + LANE
    off["z"] = _round_up(off["dt"] + LANE, ssm_inner)
    off["xbc"] = _round_up(off["z"] + ssm_inner, conv_dim)
    off["total"] = _round_up(off["xbc"] + conv_dim, 1024)
    return off


_SWAP_HALVES = np.concatenate([np.arange(ROPE_DIM // 2, ROPE_DIM), np.arange(ROPE_DIM // 2)])


def _prep_weights(w, dims):
    q_lora, kv_lora, heads, ssm_heads, hp, conv_dim = (dims[k] for k in
                                                       ("q_lora", "kv_lora", "heads", "ssm_heads", "hp", "conv_dim"))
    off = dims["off"]
    w_in = w["w_in"]
    depth, d, _ = w_in.shape
    o_kpe = q_lora + kv_lora
    o_z = o_kpe + ROPE_DIM
    o_xbc = o_z + hp
    o_dt = o_xbc + conv_dim
    kpe = w_in[..., o_kpe:o_kpe + ROPE_DIM]
    segments = [("cq", w_in[..., :q_lora]), ("ckv", w_in[..., q_lora:o_kpe]),
                ("kpe", jnp.concatenate([kpe, kpe[..., _SWAP_HALVES]], axis=-1)),
                ("dt", w_in[..., o_dt:o_dt + ssm_heads]), ("z", w_in[..., o_z:o_xbc]),
                ("xbc", w_in[..., o_xbc:o_dt])]
    parts, cursor = [], 0
    for name, seg in segments:
        if off[name] > cursor:
            parts.append(jnp.zeros((depth, d, off[name] - cursor), w_in.dtype))
        parts.append(seg)
        cursor = off[name] + seg.shape[-1]
    if off["total"] > cursor:
        parts.append(jnp.zeros((depth, d, off["total"] - cursor), w_in.dtype))
    p = {"w_in": jnp.concatenate(parts, axis=-1).astype(BF16)}

    qk = NOPE_DIM + ROPE_DIM
    uq_idx = np.concatenate([np.concatenate([h * qk + np.arange(qk), h * qk + NOPE_DIM + _SWAP_HALVES])
                             for h in range(heads)])
    p["w_uq"] = jnp.take(w["w_uq"], jnp.asarray(uq_idx, np.int32), axis=2).astype(BF16)
    p["w_ukv"] = w["w_ukv"].astype(BF16)
    p["w_o"] = w["w_o"].astype(BF16)
    p["w_xq"] = w["w_xq"].astype(BF16)
    p["w_xkv"] = jnp.concatenate([w["w_xk"], w["w_xv"]], axis=-1).astype(BF16)
    p["w_xo"] = w["w_xo"].astype(BF16)
    p["w_down"] = w["w_down"].astype(BF16)

    pad = LANE - ssm_heads
    p["dt_bias"] = jnp.pad(w["dt_bias"], ((0, 0), (0, pad)))
    p["a_log"] = jnp.pad(w["a_log"], ((0, 0), (0, pad)))
    p["d_skip_e"] = jnp.repeat(w["d_skip"], SSM_HEADDIM, axis=-1)
    return p


def _rope_table(pos, gain):
    half = ROPE_DIM // 2
    inv = ROPE_BASE ** (-jnp.arange(half, dtype=F32) / half)
    ang = pos.astype(F32)[:, None] * inv[None, :]
    cos, sin = jnp.cos(ang), jnp.sin(ang)
    g1, g2 = gain[:half][None, :], gain[half:][None, :]
    return jnp.concatenate([g1 * cos, g2 * cos, -g2 * sin, g1 * sin], axis=-1)


def _block(x, batch, pos, past_ckv, past_kpe, conv_init, h_init_t, mem_k, mem_v, l, w, p, dims):
    m, d = x.shape
    s = m // batch
    heads, q_lora, kv_lora, hp, conv_dim = (dims[k] for k in ("heads", "q_lora", "kv_lora", "hp", "conv_dim"))
    off = dims["off"]

    h = rmsnorm_bf16(x, w["norm_mix"][l])
    proj = matmul(h, p["w_in"], l, out_dtype=F32)

    tab_q = _rope_table(pos, w["qn_pe"][l])
    tab_k = _rope_table(pos, w["kn_pe"][l])
    q = q_proj(proj, off["cq"] // q_lora, q_lora, w["q_norm"][l], p["w_uq"], l, w["qn_nope"][l], tab_q, heads)
    ckv_new, kpe_new = kv_norm(proj, off["ckv"] // kv_lora, kv_lora, off["kpe"] // LANE,
                               w["kv_norm"][l], tab_k)
    if past_ckv is None:
        ckv_all, kpe_all = ckv_new, kpe_new
    else:
        ckv_all = jnp.concatenate([past_ckv, ckv_new.reshape(batch, s, kv_lora)], axis=1).reshape(-1, kv_lora)
        kpe_all = jnp.concatenate([past_kpe, kpe_new.reshape(batch, s, ROPE_DIM)], axis=1).reshape(-1, ROPE_DIM)
    k, v = kv_up(ckv_all, kpe_all, p["w_ukv"], l, w["kn_nope"][l], heads)
    attn = flash_attention(q, k, v, batch)

    y, h_new_t = ssd_mixer(proj, batch, off["xbc"] // conv_dim, conv_dim, off["z"] // hp, hp,
                           off["dt"] // LANE, conv_init, h_init_t, w["conv_w"][l],
                           w["conv_b"][l].reshape(1, -1), p["dt_bias"][l].reshape(1, -1),
                           p["a_log"][l].reshape(1, -1), p["d_skip_e"][l].reshape(1, -1),
                           w["ssm_norm"][l].reshape(1, -1), dims["e_mat"], dims["groups"], dims["rep"])
    conv_new = proj.reshape(batch, s, -1)[:, s - (CONV_W - 1):, off["xbc"]:off["xbc"] + conv_dim]

    x = matmul_concat_residual(attn, y, p["w_o"], l, x)
    x, h_ffn = cross_attention_block(x, batch, w["norm_xa"][l], p["w_xq"], w["xa_norm_q"][l],
                                     mem_k, mem_v, p["w_xo"], l, w["norm_ffn"][l], dims["xa_heads"])
    hid = swiglu_up(h_ffn, w["w_gate"], w["w_up"], l)
    f = hid.shape[1]
    x = matmul(hid, p["w_down"], l, out_dtype=F32, residual=x, tm=1024, tn=512,
               tk=f // 2 if (f // 2) % LANE == 0 else None)
    return x, ckv_new, kpe_new, conv_new, h_new_t


def _state_to_t(h, groups, rep):
    b = h.shape[0]
    return h.reshape(b, groups, rep * SSM_HEADDIM, D_STATE).transpose(0, 1, 3, 2)


def _state_from_t(ht, groups, rep):
    b = ht.shape[0]
    return ht.transpose(0, 1, 3, 2).reshape(b, groups * rep, SSM_HEADDIM, D_STATE)


def kernel(x_prompt, x_sample, mem_prompt, cache_ckv, cache_kpe, state_conv, state_ssm, cache_mem_k, cache_mem_v, norm_mix, w_in, q_norm, w_uq, kv_norm, w_ukv, qn_nope, qn_pe, kn_nope, kn_pe, conv_w, conv_b, dt_bias, a_log, d_skip, ssm_norm, w_o, norm_xa, mem_norm, w_xq, w_xk, w_xv, xa_norm_q, xa_norm_k, w_xo, norm_ffn, w_gate, w_up, w_down):
    w = dict(norm_mix=norm_mix, w_in=w_in, q_norm=q_norm, w_uq=w_uq, kv_norm=kv_norm, w_ukv=w_ukv,
             qn_nope=qn_nope, qn_pe=qn_pe, kn_nope=kn_nope, kn_pe=kn_pe, conv_w=conv_w, conv_b=conv_b,
             dt_bias=dt_bias, a_log=a_log, d_skip=d_skip, ssm_norm=ssm_norm, w_o=w_o, norm_xa=norm_xa,
             mem_norm=mem_norm, w_xq=w_xq, w_xk=w_xk, w_xv=w_xv, xa_norm_q=xa_norm_q,
             xa_norm_k=xa_norm_k, w_xo=w_xo, norm_ffn=norm_ffn, w_gate=w_gate, w_up=w_up, w_down=w_down)
    bp, sp, d = x_prompt.shape
    bs, ss, _ = x_sample.shape
    depth = w_in.shape[0]
    past = cache_ckv.shape[2]
    q_lora, kv_lora = q_norm.shape[-1], kv_norm.shape[-1]
    heads = w_uq.shape[-1] // (NOPE_DIM + ROPE_DIM)
    ssm_heads, hp, conv_dim = dt_bias.shape[-1], ssm_norm.shape[-1], conv_w.shape[-1]
    groups = (conv_dim - hp) // (2 * D_STATE)
    rep = ssm_heads // groups
    xa_heads = w_xq.shape[-1] // XA_HEAD_DIM
    n_mem = mem_prompt.shape[1]
    assert ssm_heads <= LANE and hp == ssm_heads * SSM_HEADDIM and sp % CHUNK == 0 and ss % CHUNK == 0

    e_mat = (np.arange(LANE)[:, None] == (np.arange(hp) // SSM_HEADDIM)[None, :]).astype(np.float32)
    dims = dict(q_lora=q_lora, kv_lora=kv_lora, heads=heads, ssm_heads=ssm_heads, hp=hp,
                conv_dim=conv_dim, groups=groups, rep=rep, xa_heads=xa_heads,
                off=_layout(q_lora, kv_lora, hp, conv_dim), e_mat=jnp.asarray(e_mat))
    p = _prep_weights(w, dims)

    pos_p = jnp.arange(sp, dtype=jnp.int32)
    pos_s = past + jnp.arange(ss, dtype=jnp.int32)
    conv0 = jnp.zeros((bp, CONV_W - 1, conv_dim), F32)
    ssm0_t = jnp.zeros((bp, groups, D_STATE, rep * SSM_HEADDIM), F32)

    hp_x = x_prompt.reshape(bp * sp, d)
    hs_x = x_sample.reshape(bs * ss, d)
    mem2d = mem_prompt.reshape(bp * n_mem, d)
    outs = {k: [] for k in ("ckv_p", "kpe_p", "conv_p", "ssm_p", "mk_p", "mv_p",
                            "ckv_s", "kpe_s", "conv_s", "ssm_s")}
    for l in range(depth):
        mk, mv = memory_kv(mem2d, mem_norm[l], p["w_xkv"], l, xa_norm_k[l], xa_heads)
        mk = mk.reshape(bp, n_mem, -1)
        mv = mv.reshape(bp, n_mem, -1)
        hp_x, a, b, c, dd = _block(hp_x, bp, pos_p, None, None, conv0, ssm0_t, mk, mv, l, w, p, dims)
        outs["ckv_p"].append(a.reshape(bp, sp, kv_lora))
        outs["kpe_p"].append(b.reshape(bp, sp, ROPE_DIM))
        outs["conv_p"].append(c)
        outs["ssm_p"].append(_state_from_t(dd, groups, rep))
        outs["mk_p"].append(mk.reshape(bp, n_mem, xa_heads, XA_HEAD_DIM))
        outs["mv_p"].append(mv.reshape(bp, n_mem, xa_heads, XA_HEAD_DIM))
        hs_x, a, b, c, dd = _block(hs_x, bs, pos_s, cache_ckv[l], cache_kpe[l], state_conv[l],
                                   _state_to_t(state_ssm[l], groups, rep),
                                   cache_mem_k[l].reshape(bs, n_mem, -1),
                                   cache_mem_v[l].reshape(bs, n_mem, -1), l, w, p, dims)
        outs["ckv_s"].append(a.reshape(bs, ss, kv_lora))
        outs["kpe_s"].append(b.reshape(bs, ss, ROPE_DIM))
        outs["conv_s"].append(c)
        outs["ssm_s"].append(_state_from_t(dd, groups, rep))

    st = lambda k: jnp.stack(outs[k])
    return (hp_x.reshape(bp, sp, d), hs_x.reshape(bs, ss, d),
            st("ckv_p"), st("kpe_p"), st("conv_p"), st("ssm_p"), st("mk_p"), st("mv_p"),
            st("ckv_s"), st("kpe_s"), st("conv_s"), st("ssm_s"))
```

```python
import functools

import numpy as np
import jax
import jax.numpy as jnp
from jax import lax
from jax.experimental import pallas as pl
from jax.experimental.pallas import tpu as pltpu

F32 = jnp.float32
BF16 = jnp.bfloat16
HIGHEST = lax.Precision.HIGHEST

EPS = 1e-6
CHUNK = 64
CHUNK_SHIFT = 6
NOPE_DIM = 128
ROPE_DIM = 64
V_DIM = 128
QK_SLAB = 256
V_SLAB = 256
ROPE_BASE = 10000.0
MLA_SCALE = (NOPE_DIM + ROPE_DIM) ** -0.5
SSM_HEADDIM = 64
D_STATE = 128
CONV_W = 4
XA_HEAD_DIM = 128
XA_SCALE = XA_HEAD_DIM ** -0.5
LANE = 128
VMEM_LIMIT = 56 * 1024 * 1024


def _round_up(n, m):
    return -(-n // m) * m


def _tile(n, pref, mult=8):
    if n <= pref:
        return n
    t = pref - pref % mult
    while t >= mult:
        if n % t == 0:
            return t
        t -= mult
    return n


def _params(*sem):
    return pltpu.CompilerParams(dimension_semantics=sem, vmem_limit_bytes=VMEM_LIMIT)


def _resident(shape, index_map):
    return pl.BlockSpec(shape, index_map, pipeline_mode=pl.Buffered(1))


def _rms(x, width):
    return lax.rsqrt(jnp.sum(x * x, axis=-1, keepdims=True) * (1.0 / width) + EPS)


def _silu(x):
    return x / (1.0 + jnp.exp(-x))


def _rmsnorm_kernel(x_ref, g_ref, o_ref):
    x = x_ref[...]
    o_ref[...] = (x * _rms(x, x.shape[-1]) * g_ref[...]).astype(o_ref.dtype)


def rmsnorm_bf16(x, g):
    m, d = x.shape
    tm = _tile(m, 256)
    return pl.pallas_call(
        _rmsnorm_kernel,
        grid=(m // tm,),
        in_specs=[pl.BlockSpec((tm, d), lambda i: (i, 0)), pl.BlockSpec((1, d), lambda i: (0, 0))],
        out_specs=pl.BlockSpec((tm, d), lambda i: (i, 0)),
        out_shape=jax.ShapeDtypeStruct((m, d), BF16),
        compiler_params=_params("parallel"),
        name="rmsnorm_bf16",
    )(x, g.reshape(1, d))


def _mm_kernel(*refs, nk, has_res):
    if has_res:
        a_ref, b_ref, r_ref, o_ref = refs[:4]
    else:
        a_ref, b_ref, o_ref = refs[:3]
        r_ref = None

    def part():
        return jnp.dot(a_ref[...], b_ref[...], preferred_element_type=F32)

    def finish(total):
        if has_res:
            total = total + r_ref[...]
        o_ref[...] = total.astype(o_ref.dtype)

    if nk == 1:
        finish(part())
        return
    acc_ref = refs[-1]
    k = pl.program_id(2)

    @pl.when(k == 0)
    def _():
        acc_ref[...] = part()

    if nk > 2:
        @pl.when(jnp.logical_and(k > 0, k < nk - 1))
        def _():
            acc_ref[...] += part()

    @pl.when(k == nk - 1)
    def _():
        finish(acc_ref[...] + part())


def matmul(a, b, layer, *, out_dtype, residual=None, tm=1024, tn=1024, tk=None):
    m, kd = a.shape
    n = b.shape[-1]
    tm = _tile(m, tm)
    tn = _tile(n, tn, LANE)
    tk = kd if tk is None else _tile(kd, tk, LANE)
    nk = kd // tk
    in_specs = [pl.BlockSpec((tm, tk), lambda i, j, k: (i, k)),
                pl.BlockSpec((None, tk, tn), lambda i, j, k: (layer, k, j))]
    args = [a, b]
    if residual is not None:
        in_specs.append(pl.BlockSpec((tm, tn), lambda i, j, k: (i, j)))
        args.append(residual)
    scratch = [pltpu.VMEM((tm, tn), F32)] if nk > 1 else []
    return pl.pallas_call(
        functools.partial(_mm_kernel, nk=nk, has_res=residual is not None),
        grid=(m // tm, n // tn, nk),
        in_specs=in_specs,
        out_specs=pl.BlockSpec((tm, tn), lambda i, j, k: (i, j)),
        out_shape=jax.ShapeDtypeStruct((m, n), out_dtype),
        scratch_shapes=scratch,
        compiler_params=_params("parallel", "parallel", "arbitrary"),
        name="matmul",
    )(*args)


def _mm2_kernel(a1_ref, a2_ref, b_ref, r_ref, o_ref):
    k1 = a1_ref.shape[1]
    acc = jnp.dot(a1_ref[...], b_ref[:k1, :], preferred_element_type=F32)
    acc = acc + jnp.dot(a2_ref[...], b_ref[k1:, :], preferred_element_type=F32)
    o_ref[...] = acc + r_ref[...]


def matmul_concat_residual(a1, a2, b, layer, residual, *, tm=1024, tn=512):
    m, k1 = a1.shape
    k2 = a2.shape[1]
    n = b.shape[-1]
    tm = _tile(m, tm)
    tn = _tile(n, tn, LANE)
    return pl.pallas_call(
        _mm2_kernel,
        grid=(m // tm, n // tn),
        in_specs=[pl.BlockSpec((tm, k1), lambda i, j: (i, 0)),
                  pl.BlockSpec((tm, k2), lambda i, j: (i, 0)),
                  pl.BlockSpec((None, k1 + k2, tn), lambda i, j: (layer, 0, j)),
                  pl.BlockSpec((tm, tn), lambda i, j: (i, j))],
        out_specs=pl.BlockSpec((tm, tn), lambda i, j: (i, j)),
        out_shape=jax.ShapeDtypeStruct((m, n), F32),
        compiler_params=_params("parallel", "parallel"),
        name="matmul_concat_residual",
    )(a1, a2, b, residual)


def _swiglu_kernel(a_ref, wg_ref, wu_ref, o_ref):
    a = a_ref[...]
    g = jnp.dot(a, wg_ref[...].astype(BF16), preferred_element_type=F32)
    u = jnp.dot(a, wu_ref[...].astype(BF16), preferred_element_type=F32)
    o_ref[...] = (_silu(g) * u).astype(o_ref.dtype)


def swiglu_up(a, wg, wu, layer, *, tm=2048, tn=256):
    m, d = a.shape
    f = wg.shape[-1]
    tm = _tile(m, tm)
    tn = _tile(f, tn, LANE)
    w_spec = pl.BlockSpec((None, d, tn), lambda i, j: (layer, 0, j))
    return pl.pallas_call(
        _swiglu_kernel,
        grid=(m // tm, f // tn),
        in_specs=[_resident((tm, d), lambda i, j: (i, 0)), w_spec, w_spec],
        out_specs=pl.BlockSpec((tm, tn), lambda i, j: (i, j)),
        out_shape=jax.ShapeDtypeStruct((m, f), BF16),
        compiler_params=_params("parallel", "parallel"),
        name="swiglu_up",
    )(a, wg, wu)


def _q_kernel(cq_ref, gq_ref, w_ref, gn_ref, tab_ref, q_ref, *, heads):
    x = cq_ref[...]
    xn = (x * _rms(x, x.shape[-1]) * gq_ref[...]).astype(BF16)
    tab = tab_ref[...]
    gn = gn_ref[...]
    for h in range(heads):
        acc = jnp.dot(xn, w_ref[:, h * QK_SLAB:(h + 1) * QK_SLAB], preferred_element_type=F32)
        nope = acc[:, :NOPE_DIM]
        rot = acc[:, NOPE_DIM:]
        qn = nope * _rms(nope, NOPE_DIM) * gn
        qr = rot * _rms(rot, 2 * ROPE_DIM) * tab
        q_ref[h, :, :NOPE_DIM] = (qn * MLA_SCALE).astype(BF16)
        q_ref[h, :, NOPE_DIM:] = (qr * MLA_SCALE).astype(BF16)


def q_proj(proj, col_blk, q_lora, gq, w_uq_r, layer, gn, tab, heads):
    m = proj.shape[0]
    s_tab = tab.shape[0]
    tm = _tile(s_tab, 256)
    nt = s_tab // tm
    return pl.pallas_call(
        functools.partial(_q_kernel, heads=heads),
        grid=(m // tm,),
        in_specs=[pl.BlockSpec((tm, q_lora), lambda i: (i, col_blk)),
                  pl.BlockSpec((1, q_lora), lambda i: (0, 0)),
                  _resident((None, q_lora, heads * QK_SLAB), lambda i: (layer, 0, 0)),
                  pl.BlockSpec((1, NOPE_DIM), lambda i: (0, 0)),
                  pl.BlockSpec((tm, 2 * ROPE_DIM), lambda i: (i % nt, 0))],
        out_specs=pl.BlockSpec((heads, tm, QK_SLAB), lambda i: (0, i, 0)),
        out_shape=jax.ShapeDtypeStruct((heads, m, QK_SLAB), BF16),
        compiler_params=_params("parallel"),
        name="q_proj",
    )(proj, gq.reshape(1, -1), w_uq_r, gn.reshape(1, -1), tab)


def _kv_norm_kernel(ckv_ref, kpe_ref, g_ref, tab_ref, ckv_o, kpe_o):
    x = ckv_ref[...]
    ckv_o[...] = x * _rms(x, x.shape[-1]) * g_ref[...]
    t = kpe_ref[...]
    u = t * _rms(t, 2 * ROPE_DIM) * tab_ref[...]
    kpe_o[...] = u[:, :ROPE_DIM] + u[:, ROPE_DIM:]


def kv_norm(proj, ckv_blk, kv_lora, kpe_blk, g, tab):
    m = proj.shape[0]
    s_tab = tab.shape[0]
    tm = _tile(s_tab, 512)
    nt = s_tab // tm
    return pl.pallas_call(
        _kv_norm_kernel,
        grid=(m // tm,),
        in_specs=[pl.BlockSpec((tm, kv_lora), lambda i: (i, ckv_blk)),
                  pl.BlockSpec((tm, 2 * ROPE_DIM), lambda i: (i, kpe_blk)),
                  pl.BlockSpec((1, kv_lora), lambda i: (0, 0)),
                  pl.BlockSpec((tm, 2 * ROPE_DIM), lambda i: (i % nt, 0))],
        out_specs=[pl.BlockSpec((tm, kv_lora), lambda i: (i, 0)),
                   pl.BlockSpec((tm, ROPE_DIM), lambda i: (i, 0))],
        out_shape=[jax.ShapeDtypeStruct((m, kv_lora), F32),
                   jax.ShapeDtypeStruct((m, ROPE_DIM), F32)],
        compiler_params=_params("parallel"),
        name="kv_norm",
    )(proj, proj, g.reshape(1, -1), tab)


def _kv_up_kernel(ckv_ref, kpe_ref, w_ref, gn_ref, k_ref, v_ref, *, heads):
    xb = ckv_ref[...].astype(BF16)
    kp = kpe_ref[...]
    kp2 = jnp.concatenate([kp, kp], axis=-1).astype(BF16)
    gn = gn_ref[...]
    ones = jnp.ones((xb.shape[0], V_SLAB - V_DIM), BF16)
    for h in range(heads):
        acc = jnp.dot(xb, w_ref[:, h * 256:(h + 1) * 256], preferred_element_type=F32)
        nope = acc[:, :NOPE_DIM]
        k_ref[h, :, :NOPE_DIM] = (nope * _rms(nope, NOPE_DIM) * gn).astype(BF16)
        k_ref[h, :, NOPE_DIM:] = kp2
        v_ref[h, :, :V_DIM] = acc[:, NOPE_DIM:].astype(BF16)
        v_ref[h, :, V_DIM:] = ones


def kv_up(ckv, kpe, w_ukv, layer, gn, heads):
    m, kv_lora = ckv.shape
    tm = _tile(m, 256)
    return pl.pallas_call(
        functools.partial(_kv_up_kernel, heads=heads),
        grid=(m // tm,),
        in_specs=[pl.BlockSpec((tm, kv_lora), lambda i: (i, 0)),
                  pl.BlockSpec((tm, ROPE_DIM), lambda i: (i, 0)),
                  _resident((None, kv_lora, heads * 256), lambda i: (layer, 0, 0)),
                  pl.BlockSpec((1, NOPE_DIM), lambda i: (0, 0))],
        out_specs=[pl.BlockSpec((heads, tm, QK_SLAB), lambda i: (0, i, 0)),
                   pl.BlockSpec((heads, tm, V_SLAB), lambda i: (0, i, 0))],
        out_shape=[jax.ShapeDtypeStruct((heads, m, QK_SLAB), BF16),
                   jax.ShapeDtypeStruct((heads, m, V_SLAB), BF16)],
        compiler_params=_params("parallel"),
        name="kv_up",
    )(ckv, kpe, w_ukv, gn.reshape(1, -1))


def _flash_kernel(q_ref, k_ref, v_ref, o_ref, m_ref, acc_ref, *, heads, tq, tk, nk, q_off, unroll):
    i = pl.program_id(1)
    j = pl.program_id(2)
    q_lo = q_off + i * tq
    j_last = jnp.minimum((((q_lo + tq - 1) // CHUNK + 1) * CHUNK - 1) // tk, nk - 1)
    fully_visible = ((j * tk + tk - 1) // CHUNK) <= (q_lo // CHUNK)

    @pl.when(j == 0)
    def _():
        m_ref[...] = jnp.full(m_ref.shape, -jnp.inf, F32)
        acc_ref[...] = jnp.zeros(acc_ref.shape, F32)

    def tile(masked):
        if masked:
            q_chunk = (q_lo + lax.broadcasted_iota(jnp.int32, (tq, tk), 0)) >> CHUNK_SHIFT
            k_chunk = (j * tk + lax.broadcasted_iota(jnp.int32, (tq, tk), 1)) >> CHUNK_SHIFT
            visible = k_chunk <= q_chunk

        def head(h, carry):
            s = lax.dot_general(q_ref[h], k_ref[h], (((1,), (1,)), ((), ())),
                                preferred_element_type=F32)
            if masked:
                s = jnp.where(visible, s, -jnp.inf)
            m_prev = m_ref[h]
            m_new = jnp.maximum(m_prev, jnp.max(s, axis=-1, keepdims=True))
            alpha = jnp.exp(m_prev - m_new)
            m_b = pltpu.repeat(m_new, tk // LANE, axis=1) if tk % LANE == 0 else m_new[:, :1]
            p = jnp.exp(s - m_b)
            pv = jnp.dot(p.astype(BF16), v_ref[h], preferred_element_type=F32)
            acc_ref[h] = pltpu.repeat(alpha, V_SLAB // LANE, axis=1) * acc_ref[h] + pv
            m_ref[h] = m_new
            return carry

        lax.fori_loop(0, heads, head, 0, unroll=unroll)

    @pl.when(jnp.logical_and(j <= j_last, fully_visible))
    def _():
        tile(False)

    @pl.when(jnp.logical_and(j <= j_last, jnp.logical_not(fully_visible)))
    def _():
        tile(True)

    @pl.when(j == nk - 1)
    def _():
        for h in range(heads):
            acc = acc_ref[h]
            o_ref[:, h * V_DIM:(h + 1) * V_DIM] = (acc[:, :V_DIM] / acc[:, V_DIM:]).astype(o_ref.dtype)


def flash_attention(q, k, v, batch, *, tq=512, tk=512, unroll=8):
    heads, mq, _ = q.shape
    mk = k.shape[1]
    sq, sk = mq // batch, mk // batch
    tq = _tile(sq, tq)
    tk = _tile(sk, tk, 16)
    nq, nk = sq // tq, sk // tk
    q_off = sk - sq
    unroll = min(unroll, heads)

    def kv_index(b, i, j):
        j_last = (((q_off + i * tq + tq - 1) // CHUNK + 1) * CHUNK - 1) // tk
        return (0, b * nk + jnp.minimum(j, jnp.minimum(j_last, nk - 1)), 0)

    return pl.pallas_call(
        functools.partial(_flash_kernel, heads=heads, tq=tq, tk=tk, nk=nk, q_off=q_off, unroll=unroll),
        grid=(batch, nq, nk),
        in_specs=[pl.BlockSpec((heads, tq, QK_SLAB), lambda b, i, j: (0, b * nq + i, 0)),
                  pl.BlockSpec((heads, tk, QK_SLAB), kv_index),
                  pl.BlockSpec((heads, tk, V_SLAB), kv_index)],
        out_specs=pl.BlockSpec((tq, heads * V_DIM), lambda b, i, j: (b * nq + i, 0)),
        out_shape=jax.ShapeDtypeStruct((mq, heads * V_DIM), BF16),
        scratch_shapes=[pltpu.VMEM((heads, tq, LANE), F32),
                        pltpu.VMEM((heads, tq, V_SLAB), F32)],
        compiler_params=_params("parallel", "parallel", "arbitrary"),
        name="flash_attention",
    )(q, k, v)


def _expand_heads(x, g, rep):
    q = x.shape[0]
    halves = []
    for t in range(rep * SSM_HEADDIM // LANE):
        lane = lax.broadcasted_iota(jnp.int32, (q, LANE), 1)
        idx = g * rep + t * (LANE // SSM_HEADDIM) + (lane >> CHUNK_SHIFT)
        halves.append(jnp.take_along_axis(x, idx, axis=1))
    return jnp.concatenate(halves, axis=1)


def _ssd_kernel(xbc_ref, z_ref, dt_ref, cinit_ref, hinit_ref, cw_ref, cb_ref, dtb_ref, alog_ref,
                dsk_ref, gn_ref, bd_ref, y_ref, hout_ref, buf, ht, xs_s, bm_s, cm_s, *, groups, rep, nc):
    c = pl.program_id(1)
    q = CHUNK
    gw = rep * SSM_HEADDIM
    hp = groups * gw
    halo = CONV_W - 1
    base = 8
    conv_dim = buf.shape[1]

    @pl.when(c == 0)
    def _():
        buf[base - halo:base, :] = cinit_ref[0]
        ht[...] = hinit_ref[0]

    buf[base:base + q, :] = xbc_ref[...]
    blk = 2 * LANE
    for j in range(conv_dim // blk):
        cols = slice(j * blk, (j + 1) * blk)
        xall = buf[:, cols]
        conv = cb_ref[:, cols]
        for i in range(CONV_W):
            shifted = xall if i == halo else pltpu.roll(xall, halo - i, 0)
            conv = conv + shifted[base:base + q, :] * cw_ref[i:i + 1, cols]
        xc = _silu(conv)
        lo = j * blk
        if lo < hp:
            xs_s[:, lo:lo + blk] = xc
        elif lo < hp + groups * D_STATE:
            bm_s[:, lo - hp:lo - hp + blk] = xc.astype(BF16)
        else:
            o = lo - hp - groups * D_STATE
            cm_s[:, o:o + blk] = xc.astype(BF16)
    tail = buf[base + q - halo:base + q, :]
    buf[base - halo:base, :] = tail

    dt_in = dt_ref[...] + dtb_ref[...]
    dt = jnp.maximum(dt_in, 0.0) + jnp.log1p(jnp.exp(-jnp.abs(dt_in)))
    a = dt * (-jnp.exp(alog_ref[...]))
    row = lax.broadcasted_iota(jnp.int32, (q, q), 0)
    col = lax.broadcasted_iota(jnp.int32, (q, q), 1)
    tri = (col <= row).astype(F32)
    a_cum = jnp.dot(tri, a, precision=HIGHEST, preferred_element_type=F32)

    r_idx = lax.broadcasted_iota(jnp.int32, (q, gw), 0)
    l_idx = lax.broadcasted_iota(jnp.int32, (q, gw), 1) & (q - 1)
    on_diag = l_idx == r_idx
    causal = l_idx <= r_idx
    bd = bd_ref[...]

    for g in range(groups):
        lanes = slice(g * gw, (g + 1) * gw)
        a_col = _expand_heads(a_cum, g, rep)
        dt_col = _expand_heads(dt, g, rep)
        a_row = jnp.sum(jnp.where(on_diag, a_col, 0.0), axis=0, keepdims=True)
        decay = jnp.exp(jnp.where(causal, a_col - a_row, -jnp.inf))
        a_last = a_col[q - 1:q, :]
        xs = xs_s[:, lanes]
        xdt = xs * dt_col
        xw = (xdt * jnp.exp(a_last - a_col)).astype(BF16)
        bg = bm_s[:, g * D_STATE:(g + 1) * D_STATE]
        cg = cm_s[:, g * D_STATE:(g + 1) * D_STATE]
        b_rep = jnp.concatenate([bg] * rep, axis=0)
        cb = lax.dot_general(cg, b_rep, (((1,), (1,)), ((), ())), preferred_element_type=F32)
        mg = (cb * decay).astype(BF16)
        x_blk = jnp.concatenate([xdt.astype(BF16)] * rep, axis=0) * bd
        y_diag = jnp.dot(mg, x_blk, preferred_element_type=F32)
        h_in = ht[g]
        y_off = jnp.dot(cg, h_in.astype(BF16), preferred_element_type=F32) * jnp.exp(a_col)
        st = lax.dot_general(bg, xw, (((0,), (0,)), ((), ())), preferred_element_type=F32)
        ht[g] = h_in * jnp.exp(a_last) + st
        yg = (y_diag + y_off + dsk_ref[:, lanes] * xs) * _silu(z_ref[:, lanes])
        y_ref[:, lanes] = (yg * _rms(yg, gw) * gn_ref[:, lanes]).astype(y_ref.dtype)

    @pl.when(c == nc - 1)
    def _():
        hout_ref[0] = ht[...]


def ssd_mixer(proj, batch, xbc_blk, conv_dim, z_blk, hp, dt_blk, conv_init, h_init_t, conv_w,
              conv_b, dtb, alog, dsk_e, gn, bd_mask, groups, rep):
    m = proj.shape[0]
    nc = m // batch // CHUNK
    gw = rep * SSM_HEADDIM
    const2 = lambda b, c: (0, 0)
    return pl.pallas_call(
        functools.partial(_ssd_kernel, groups=groups, rep=rep, nc=nc),
        grid=(batch, nc),
        in_specs=[pl.BlockSpec((CHUNK, conv_dim), lambda b, c: (b * nc + c, xbc_blk)),
                  pl.BlockSpec((CHUNK, hp), lambda b, c: (b * nc + c, z_blk)),
                  pl.BlockSpec((CHUNK, LANE), lambda b, c: (b * nc + c, dt_blk)),
                  pl.BlockSpec((1, CONV_W - 1, conv_dim), lambda b, c: (b, 0, 0)),
                  pl.BlockSpec((1, groups, D_STATE, gw), lambda b, c: (b, 0, 0, 0)),
                  pl.BlockSpec((CONV_W, conv_dim), const2),
                  pl.BlockSpec((1, conv_dim), const2),
                  pl.BlockSpec((1, LANE), const2),
                  pl.BlockSpec((1, LANE), const2),
                  pl.BlockSpec((1, hp), const2),
                  pl.BlockSpec((1, hp), const2),
                  pl.BlockSpec((rep * CHUNK, gw), const2)],
        out_specs=[pl.BlockSpec((CHUNK, hp), lambda b, c: (b * nc + c, 0)),
                   pl.BlockSpec((1, groups, D_STATE, gw), lambda b, c: (b, 0, 0, 0))],
        out_shape=[jax.ShapeDtypeStruct((m, hp), BF16),
                   jax.ShapeDtypeStruct((batch, groups, D_STATE, gw), F32)],
        scratch_shapes=[pltpu.VMEM((8 + CHUNK, conv_dim), F32),
                        pltpu.VMEM((groups, D_STATE, gw), F32),
                        pltpu.VMEM((CHUNK, hp), F32),
                        pltpu.VMEM((CHUNK, groups * D_STATE), BF16),
                        pltpu.VMEM((CHUNK, groups * D_STATE), BF16)],
        compiler_params=_params("parallel", "arbitrary"),
        name="ssd_mixer",
    )(proj, proj, proj, conv_init, h_init_t, conv_w, conv_b, dtb, alog, dsk_e, gn, bd_mask)


def _mem_kv_kernel(x_ref, g_ref, w_ref, gk_ref, k_ref, v_ref, *, heads):
    x = x_ref[...]
    xn = (x * _rms(x, x.shape[-1]) * g_ref[...]).astype(BF16)
    width = heads * XA_HEAD_DIM
    kv = jnp.dot(xn, w_ref[...], preferred_element_type=F32)
    for h in range(heads):
        kh = kv[:, h * XA_HEAD_DIM:(h + 1) * XA_HEAD_DIM]
        k_ref[:, h * XA_HEAD_DIM:(h + 1) * XA_HEAD_DIM] = kh * _rms(kh, XA_HEAD_DIM) * gk_ref[...]
    v_ref[...] = kv[:, width:]


def memory_kv(mem, g, w_kv, layer, gk, heads):
    m, d = mem.shape
    width = heads * XA_HEAD_DIM
    tm = _tile(m, 256)
    return pl.pallas_call(
        functools.partial(_mem_kv_kernel, heads=heads),
        grid=(m // tm,),
        in_specs=[pl.BlockSpec((tm, d), lambda i: (i, 0)),
                  pl.BlockSpec((1, d), lambda i: (0, 0)),
                  _resident((None, d, 2 * width), lambda i: (layer, 0, 0)),
                  pl.BlockSpec((1, XA_HEAD_DIM), lambda i: (0, 0))],
        out_specs=[pl.BlockSpec((tm, width), lambda i: (i, 0)),
                   pl.BlockSpec((tm, width), lambda i: (i, 0))],
        out_shape=[jax.ShapeDtypeStruct((m, width), F32), jax.ShapeDtypeStruct((m, width), F32)],
        compiler_params=_params("parallel"),
        name="memory_kv",
    )(mem, g.reshape(1, d), w_kv, gk.reshape(1, -1))


def _xattn_kernel(x_ref, gxa_ref, wq_ref, gq_ref, mk_ref, mv_ref, wo_ref, gffn_ref, xo_ref, ho_ref,
                  *, heads):
    x = x_ref[...]
    d = x.shape[-1]
    xn = (x * _rms(x, d) * gxa_ref[...]).astype(BF16)
    q = jnp.dot(xn, wq_ref[...], preferred_element_type=F32)
    outs = []
    for h in range(heads):
        cols = slice(h * XA_HEAD_DIM, (h + 1) * XA_HEAD_DIM)
        qh = q[:, cols]
        qh = (qh * _rms(qh, XA_HEAD_DIM) * gq_ref[...] * XA_SCALE).astype(BF16)
        kh = mk_ref[0, :, cols].astype(BF16)
        vh = mv_ref[0, :, cols].astype(BF16)
        s = lax.dot_general(qh, kh, (((1,), (1,)), ((), ())), preferred_element_type=F32)
        p = jnp.exp(s - jnp.max(s, axis=-1, keepdims=True))
        p = p / jnp.sum(p, axis=-1, keepdims=True)
        outs.append(jnp.dot(p.astype(BF16), vh, preferred_element_type=F32))
    o = jnp.concatenate(outs, axis=-1).astype(BF16)
    x_new = x + jnp.dot(o, wo_ref[...], preferred_element_type=F32)
    xo_ref[...] = x_new
    ho_ref[...] = (x_new * _rms(x_new, d) * gffn_ref[...]).astype(ho_ref.dtype)


def cross_attention_block(x, batch, gxa, wq, gq, mem_k, mem_v, wo, layer, gffn, heads):
    m, d = x.shape
    s = m // batch
    n_mem = mem_k.shape[1]
    width = heads * XA_HEAD_DIM
    tm = _tile(s, 256)
    per_b = s // tm
    return pl.pallas_call(
        functools.partial(_xattn_kernel, heads=heads),
        grid=(m // tm,),
        in_specs=[pl.BlockSpec((tm, d), lambda i: (i, 0)),
                  pl.BlockSpec((1, d), lambda i: (0, 0)),
                  _resident((None, d, width), lambda i: (layer, 0, 0)),
                  pl.BlockSpec((1, XA_HEAD_DIM), lambda i: (0, 0)),
                  pl.BlockSpec((1, n_mem, width), lambda i: (i // per_b, 0, 0)),
                  pl.BlockSpec((1, n_mem, width), lambda i: (i // per_b, 0, 0)),
                  _resident((None, width, d), lambda i: (layer, 0, 0)),
                  pl.BlockSpec((1, d), lambda i: (0, 0))],
        out_specs=[pl.BlockSpec((tm, d), lambda i: (i, 0)),
                   pl.BlockSpec((tm, d), lambda i: (i, 0))],
        out_shape=[jax.ShapeDtypeStruct((m, d), F32), jax.ShapeDtypeStruct((m, d), BF16)],
        compiler_params=_params("parallel"),
        name="cross_attention_block",
    )(x, gxa.reshape(1, d), wq, gq.reshape(1, -1), mem_k, mem_v, wo, gffn.reshape(1, d))


def _layout(q_lora, kv_lora, ssm_inner, conv_dim):
    off = {}
    off["cq"] = 0
    off["ckv"] = _round_up(q_lora, kv_lora)
    off["kpe"] = _round_up(off["ckv"] + kv_lora, LANE)
    off["dt"] = off["kpe"] + LANE
    off["z"] = _round_up(off["dt"] + LANE, ssm_inner)
    off["xbc"] = _round_up(off["z"] + ssm_inner, conv_dim)
    off["total"] = _round_up(off["xbc"] + conv_dim, 1024)
    return off


_SWAP_HALVES = np.concatenate([np.arange(ROPE_DIM // 2, ROPE_DIM), np.arange(ROPE_DIM // 2)])


def _prep_weights(w, dims):
    q_lora, kv_lora, heads, ssm_heads, hp, conv_dim = (dims[k] for k in
                                                       ("q_lora", "kv_lora", "heads", "ssm_heads", "hp", "conv_dim"))
    off = dims["off"]
    w_in = w["w_in"]
    depth, d, _ = w_in.shape
    o_kpe = q_lora + kv_lora
    o_z = o_kpe + ROPE_DIM
    o_xbc = o_z + hp
    o_dt = o_xbc + conv_dim
    kpe = w_in[..., o_kpe:o_kpe + ROPE_DIM]
    segments = [("cq", w_in[..., :q_lora]), ("ckv", w_in[..., q_lora:o_kpe]),
                ("kpe", jnp.concatenate([kpe, kpe[..., _SWAP_HALVES]], axis=-1)),
                ("dt", w_in[..., o_dt:o_dt + ssm_heads]), ("z", w_in[..., o_z:o_xbc]),
                ("xbc", w_in[..., o_xbc:o_dt])]
    parts, cursor = [], 0
    for name, seg in segments:
        if off[name] > cursor:
            parts.append(jnp.zeros((depth, d, off[name] - cursor), w_in.dtype))
        parts.append(seg)
        cursor = off[name] + seg.shape[-1]
    if off["total"] > cursor:
        parts.append(jnp.zeros((depth, d, off["total"] - cursor), w_in.dtype))
    p = {"w_in": jnp.concatenate(parts, axis=-1).astype(BF16)}

    qk = NOPE_DIM + ROPE_DIM
    uq_idx = np.concatenate([np.concatenate([h * qk + np.arange(qk), h * qk + NOPE_DIM + _SWAP_HALVES])
                             for h in range(heads)])
    p["w_uq"] = jnp.take(w["w_uq"], jnp.asarray(uq_idx, np.int32), axis=2).astype(BF16)
    p["w_ukv"] = w["w_ukv"].astype(BF16)
    p["w_o"] = w["w_o"].astype(BF16)
    p["w_xq"] = w["w_xq"].astype(BF16)
    p["w_xkv"] = jnp.concatenate([w["w_xk"], w["w_xv"]], axis=-1).astype(BF16)
    p["w_xo"] = w["w_xo"].astype(BF16)
    p["w_down"] = w["w_down"].astype(BF16)

    pad = LANE - ssm_heads
    p["dt_bias"] = jnp.pad(w["dt_bias"], ((0, 0), (0, pad)))
    p["a_log"] = jnp.pad(w["a_log"], ((0, 0), (0, pad)))
    p["d_skip_e"] = jnp.repeat(w["d_skip"], SSM_HEADDIM, axis=-1)
    return p


def _rope_table(pos, gain):
    half = ROPE_DIM // 2
    inv = ROPE_BASE ** (-jnp.arange(half, dtype=F32) / half)
    ang = pos.astype(F32)[:, None] * inv[None, :]
    cos, sin = jnp.cos(ang), jnp.sin(ang)
    g1, g2 = gain[:half][None, :], gain[half:][None, :]
    return jnp.concatenate([g1 * cos, g2 * cos, -g2 * sin, g1 * sin], axis=-1)


def _block(x, batch, pos, past_ckv, past_kpe, conv_init, h_init_t, mem_k, mem_v, l, w, p, dims):
    m, d = x.shape
    s = m // batch
    heads, q_lora, kv_lora, hp, conv_dim = (dims[k] for k in ("heads", "q_lora", "kv_lora", "hp", "conv_dim"))
    off = dims["off"]

    h = rmsnorm_bf16(x, w["norm_mix"][l])
    proj = matmul(h, p["w_in"], l, out_dtype=F32)

    tab_q = _rope_table(pos, w["qn_pe"][l])
    tab_k = _rope_table(pos, w["kn_pe"][l])
    q = q_proj(proj, off["cq"] // q_lora, q_lora, w["q_norm"][l], p["w_uq"], l, w["qn_nope"][l], tab_q, heads)
    ckv_new, kpe_new = kv_norm(proj, off["ckv"] // kv_lora, kv_lora, off["kpe"] // LANE,
                               w["kv_norm"][l], tab_k)
    if past_ckv is None:
        ckv_all, kpe_all = ckv_new, kpe_new
    else:
        ckv_all = jnp.concatenate([past_ckv, ckv_new.reshape(batch, s, kv_lora)], axis=1).reshape(-1, kv_lora)
        kpe_all = jnp.concatenate([past_kpe, kpe_new.reshape(batch, s, ROPE_DIM)], axis=1).reshape(-1, ROPE_DIM)
    k, v = kv_up(ckv_all, kpe_all, p["w_ukv"], l, w["kn_nope"][l], heads)
    attn = flash_attention(q, k, v, batch)

    y, h_new_t = ssd_mixer(proj, batch, off["xbc"] // conv_dim, conv_dim, off["z"] // hp, hp,
                           off["dt"] // LANE, conv_init, h_init_t, w["conv_w"][l],
                           w["conv_b"][l].reshape(1, -1), p["dt_bias"][l].reshape(1, -1),
                           p["a_log"][l].reshape(1, -1), p["d_skip_e"][l].reshape(1, -1),
                           w["ssm_norm"][l].reshape(1, -1), dims["bd_mask"], dims["groups"], dims["rep"])
    conv_new = proj.reshape(batch, s, -1)[:, s - (CONV_W - 1):, off["xbc"]:off["xbc"] + conv_dim]

    x = matmul_concat_residual(attn, y, p["w_o"], l, x)
    x, h_ffn = cross_attention_block(x, batch, w["norm_xa"][l], p["w_xq"], w["xa_norm_q"][l],
                                     mem_k, mem_v, p["w_xo"], l, w["norm_ffn"][l], dims["xa_heads"])
    hid = swiglu_up(h_ffn, w["w_gate"], w["w_up"], l)
    f = hid.shape[1]
    x = matmul(hid, p["w_down"], l, out_dtype=F32, residual=x, tm=1024, tn=512,
               tk=f // 2 if (f // 2) % LANE == 0 else None)
    return x, ckv_new, kpe_new, conv_new, h_new_t


def _state_to_t(h, groups, rep):
    b = h.shape[0]
    return h.reshape(b, groups, rep * SSM_HEADDIM, D_STATE).transpose(0, 1, 3, 2)


def _state_from_t(ht, groups, rep):
    b = ht.shape[0]
    return ht.transpose(0, 1, 3, 2).reshape(b, groups * rep, SSM_HEADDIM, D_STATE)


def kernel(x_prompt, x_sample, mem_prompt, cache_ckv, cache_kpe, state_conv, state_ssm, cache_mem_k, cache_mem_v, norm_mix, w_in, q_norm, w_uq, kv_norm, w_ukv, qn_nope, qn_pe, kn_nope, kn_pe, conv_w, conv_b, dt_bias, a_log, d_skip, ssm_norm, w_o, norm_xa, mem_norm, w_xq, w_xk, w_xv, xa_norm_q, xa_norm_k, w_xo, norm_ffn, w_gate, w_up, w_down):
    w = dict(norm_mix=norm_mix, w_in=w_in, q_norm=q_norm, w_uq=w_uq, kv_norm=kv_norm, w_ukv=w_ukv,
             qn_nope=qn_nope, qn_pe=qn_pe, kn_nope=kn_nope, kn_pe=kn_pe, conv_w=conv_w, conv_b=conv_b,
             dt_bias=dt_bias, a_log=a_log, d_skip=d_skip, ssm_norm=ssm_norm, w_o=w_o, norm_xa=norm_xa,
             mem_norm=mem_norm, w_xq=w_xq, w_xk=w_xk, w_xv=w_xv, xa_norm_q=xa_norm_q,
             xa_norm_k=xa_norm_k, w_xo=w_xo, norm_ffn=norm_ffn, w_gate=w_gate, w_up=w_up, w_down=w_down)
    bp, sp, d = x_prompt.shape
    bs, ss, _ = x_sample.shape
    depth = w_in.shape[0]
    past = cache_ckv.shape[2]
    q_lora, kv_lora = q_norm.shape[-1], kv_norm.shape[-1]
    heads = w_uq.shape[-1] // (NOPE_DIM + ROPE_DIM)
    ssm_heads, hp, conv_dim = dt_bias.shape[-1], ssm_norm.shape[-1], conv_w.shape[-1]
    groups = (conv_dim - hp) // (2 * D_STATE)
    rep = ssm_heads // groups
    xa_heads = w_xq.shape[-1] // XA_HEAD_DIM
    n_mem = mem_prompt.shape[1]
    assert ssm_heads <= LANE and hp == ssm_heads * SSM_HEADDIM and sp % CHUNK == 0 and ss % CHUNK == 0

    blocks = np.arange(rep * CHUNK) // CHUNK
    bd_mask = (blocks[:, None] == (np.arange(rep * SSM_HEADDIM) // SSM_HEADDIM)[None, :]).astype(np.float32)
    dims = dict(q_lora=q_lora, kv_lora=kv_lora, heads=heads, ssm_heads=ssm_heads, hp=hp,
                conv_dim=conv_dim, groups=groups, rep=rep, xa_heads=xa_heads,
                off=_layout(q_lora, kv_lora, hp, conv_dim), bd_mask=jnp.asarray(bd_mask, BF16))
    p = _prep_weights(w, dims)

    pos_p = jnp.arange(sp, dtype=jnp.int32)
    pos_s = past + jnp.arange(ss, dtype=jnp.int32)
    conv0 = jnp.zeros((bp, CONV_W - 1, conv_dim), F32)
    ssm0_t = jnp.zeros((bp, groups, D_STATE, rep * SSM_HEADDIM), F32)

    hp_x = x_prompt.reshape(bp * sp, d)
    hs_x = x_sample.reshape(bs * ss, d)
    mem2d = mem_prompt.reshape(bp * n_mem, d)
    outs = {k: [] for k in ("ckv_p", "kpe_p", "conv_p", "ssm_p", "mk_p", "mv_p",
                            "ckv_s", "kpe_s", "conv_s", "ssm_s")}
    for l in range(depth):
        mk, mv = memory_kv(mem2d, mem_norm[l], p["w_xkv"], l, xa_norm_k[l], xa_heads)
        mk = mk.reshape(bp, n_mem, -1)
        mv = mv.reshape(bp, n_mem, -1)
        hp_x, a, b, c, dd = _block(hp_x, bp, pos_p, None, None, conv0, ssm0_t, mk, mv, l, w, p, dims)
        outs["ckv_p"].append(a.reshape(bp, sp, kv_lora))
        outs["kpe_p"].append(b.reshape(bp, sp, ROPE_DIM))
        outs["conv_p"].append(c)
        outs["ssm_p"].append(_state_from_t(dd, groups, rep))
        outs["mk_p"].append(mk.reshape(bp, n_mem, xa_heads, XA_HEAD_DIM))
        outs["mv_p"].append(mv.reshape(bp, n_mem, xa_heads, XA_HEAD_DIM))
        hs_x, a, b, c, dd = _block(hs_x, bs, pos_s, cache_ckv[l], cache_kpe[l], state_conv[l],
                                   _state_to_t(state_ssm[l], groups, rep),
                                   cache_mem_k[l].reshape(bs, n_mem, -1),
                                   cache_mem_v[l].reshape(bs, n_mem, -1), l, w, p, dims)
        outs["ckv_s"].append(a.reshape(bs, ss, kv_lora))
        outs["kpe_s"].append(b.reshape(bs, ss, ROPE_DIM))
        outs["conv_s"].append(c)
        outs["ssm_s"].append(_state_from_t(dd, groups, rep))

    st = lambda k: jnp.stack(outs[k])
    return (hp_x.reshape(bp, sp, d), hs_x.reshape(bs, ss, d),
            st("ckv_p"), st("kpe_p"), st("conv_p"), st("ssm_p"), st("mk_p"), st("mv_p"),
            st("ckv_s"), st("kpe_s"), st("conv_s"), st("ssm_s"))
```

```python
import functools

import numpy as np
import jax
import jax.numpy as jnp
from jax import lax
from jax.experimental import pallas as pl
from jax.experimental.pallas import tpu as pltpu

F32 = jnp.float32
BF16 = jnp.bfloat16
HIGHEST = lax.Precision.HIGHEST

EPS = 1e-6
CHUNK = 64
CHUNK_SHIFT = 6
NOPE_DIM = 128
ROPE_DIM = 64
V_DIM = 128
QK_SLAB = 256
V_SLAB = 256
ROPE_BASE = 10000.0
MLA_SCALE = (NOPE_DIM + ROPE_DIM) ** -0.5
SSM_HEADDIM = 64
D_STATE = 128
CONV_W = 4
XA_HEAD_DIM = 128
XA_SCALE = XA_HEAD_DIM ** -0.5
LANE = 128
VMEM_LIMIT = 56 * 1024 * 1024


def _round_up(n, m):
    return -(-n // m) * m


def _tile(n, pref, mult=8):
    if n <= pref:
        return n
    t = pref - pref % mult
    while t >= mult:
        if n % t == 0:
            return t
        t -= mult
    return n


def _params(*sem):
    return pltpu.CompilerParams(dimension_semantics=sem, vmem_limit_bytes=VMEM_LIMIT)


def _resident(shape, index_map):
    return pl.BlockSpec(shape, index_map, pipeline_mode=pl.Buffered(1))


def _rms(x, width):
    return lax.rsqrt(jnp.sum(x * x, axis=-1, keepdims=True) * (1.0 / width) + EPS)


def _silu(x):
    return x / (1.0 + jnp.exp(-x))


def _rmsnorm_kernel(x_ref, g_ref, o_ref):
    x = x_ref[...]
    o_ref[...] = (x * _rms(x, x.shape[-1]) * g_ref[...]).astype(o_ref.dtype)


def rmsnorm_bf16(x, g):
    m, d = x.shape
    tm = _tile(m, 256)
    return pl.pallas_call(
        _rmsnorm_kernel,
        grid=(m // tm,),
        in_specs=[pl.BlockSpec((tm, d), lambda i: (i, 0)), pl.BlockSpec((1, d), lambda i: (0, 0))],
        out_specs=pl.BlockSpec((tm, d), lambda i: (i, 0)),
        out_shape=jax.ShapeDtypeStruct((m, d), BF16),
        compiler_params=_params("parallel"),
        name="rmsnorm_bf16",
    )(x, g.reshape(1, d))


def _mm_kernel(*refs, nk, has_res):
    if has_res:
        a_ref, b_ref, r_ref, o_ref = refs[:4]
    else:
        a_ref, b_ref, o_ref = refs[:3]
        r_ref = None

    def part():
        return jnp.dot(a_ref[...], b_ref[...], preferred_element_type=F32)

    def finish(total):
        if has_res:
            total = total + r_ref[...]
        o_ref[...] = total.astype(o_ref.dtype)

    if nk == 1:
        finish(part())
        return
    acc_ref = refs[-1]
    k = pl.program_id(2)

    @pl.when(k == 0)
    def _():
        acc_ref[...] = part()

    if nk > 2:
        @pl.when(jnp.logical_and(k > 0, k < nk - 1))
        def _():
            acc_ref[...] += part()

    @pl.when(k == nk - 1)
    def _():
        finish(acc_ref[...] + part())


def matmul(a, b, layer, *, out_dtype, residual=None, tm=1024, tn=1024, tk=None):
    m, kd = a.shape
    n = b.shape[-1]
    tm = _tile(m, tm)
    tn = _tile(n, tn, LANE)
    tk = kd if tk is None else _tile(kd, tk, LANE)
    nk = kd // tk
    in_specs = [pl.BlockSpec((tm, tk), lambda i, j, k: (i, k)),
                pl.BlockSpec((None, tk, tn), lambda i, j, k: (layer, k, j))]
    args = [a, b]
    if residual is not None:
        in_specs.append(pl.BlockSpec((tm, tn), lambda i, j, k: (i, j)))
        args.append(residual)
    scratch = [pltpu.VMEM((tm, tn), F32)] if nk > 1 else []
    return pl.pallas_call(
        functools.partial(_mm_kernel, nk=nk, has_res=residual is not None),
        grid=(m // tm, n // tn, nk),
        in_specs=in_specs,
        out_specs=pl.BlockSpec((tm, tn), lambda i, j, k: (i, j)),
        out_shape=jax.ShapeDtypeStruct((m, n), out_dtype),
        scratch_shapes=scratch,
        compiler_params=_params("parallel", "parallel", "arbitrary"),
        name="matmul",
    )(*args)


def _mm2_kernel(a1_ref, a2_ref, b_ref, r_ref, o_ref):
    k1 = a1_ref.shape[1]
    acc = jnp.dot(a1_ref[...], b_ref[:k1, :], preferred_element_type=F32)
    acc = acc + jnp.dot(a2_ref[...], b_ref[k1:, :], preferred_element_type=F32)
    o_ref[...] = acc + r_ref[...]


def matmul_concat_residual(a1, a2, b, layer, residual, *, tm=1024, tn=512):
    m, k1 = a1.shape
    k2 = a2.shape[1]
    n = b.shape[-1]
    tm = _tile(m, tm)
    tn = _tile(n, tn, LANE)
    return pl.pallas_call(
        _mm2_kernel,
        grid=(m // tm, n // tn),
        in_specs=[pl.BlockSpec((tm, k1), lambda i, j: (i, 0)),
                  pl.BlockSpec((tm, k2), lambda i, j: (i, 0)),
                  pl.BlockSpec((None, k1 + k2, tn), lambda i, j: (layer, 0, j)),
                  pl.BlockSpec((tm, tn), lambda i, j: (i, j))],
        out_specs=pl.BlockSpec((tm, tn), lambda i, j: (i, j)),
        out_shape=jax.ShapeDtypeStruct((m, n), F32),
        compiler_params=_params("parallel", "parallel"),
        name="matmul_concat_residual",
    )(a1, a2, b, residual)


def _swiglu_kernel(a_ref, wg_ref, wu_ref, o_ref):
    a = a_ref[...]
    g = jnp.dot(a, wg_ref[...].astype(BF16), preferred_element_type=F32)
    u = jnp.dot(a, wu_ref[...].astype(BF16), preferred_element_type=F32)
    o_ref[...] = (_silu(g) * u).astype(o_ref.dtype)


def swiglu_up(a, wg, wu, layer, *, tm=2048, tn=256):
    m, d = a.shape
    f = wg.shape[-1]
    tm = _tile(m, tm)
    tn = _tile(f, tn, LANE)
    w_spec = pl.BlockSpec((None, d, tn), lambda i, j: (layer, 0, j))
    return pl.pallas_call(
        _swiglu_kernel,
        grid=(m // tm, f // tn),
        in_specs=[_resident((tm, d), lambda i, j: (i, 0)), w_spec, w_spec],
        out_specs=pl.BlockSpec((tm, tn), lambda i, j: (i, j)),
        out_shape=jax.ShapeDtypeStruct((m, f), BF16),
        compiler_params=_params("parallel", "parallel"),
        name="swiglu_up",
    )(a, wg, wu)


def _q_kernel(cq_ref, gq_ref, w_ref, gn_ref, tab_ref, q_ref, *, heads):
    x = cq_ref[...]
    xn = (x * _rms(x, x.shape[-1]) * gq_ref[...]).astype(BF16)
    tab = tab_ref[...]
    gn = gn_ref[...]
    for h in range(heads):
        acc = jnp.dot(xn, w_ref[:, h * QK_SLAB:(h + 1) * QK_SLAB], preferred_element_type=F32)
        nope = acc[:, :NOPE_DIM]
        rot = acc[:, NOPE_DIM:]
        qn = nope * _rms(nope, NOPE_DIM) * gn
        qr = rot * _rms(rot, 2 * ROPE_DIM) * tab
        q_ref[h, :, :NOPE_DIM] = (qn * MLA_SCALE).astype(BF16)
        q_ref[h, :, NOPE_DIM:] = (qr * MLA_SCALE).astype(BF16)


def q_proj(proj, col_blk, q_lora, gq, w_uq_r, layer, gn, tab, heads):
    m = proj.shape[0]
    s_tab = tab.shape[0]
    tm = _tile(s_tab, 256)
    nt = s_tab // tm
    return pl.pallas_call(
        functools.partial(_q_kernel, heads=heads),
        grid=(m // tm,),
        in_specs=[pl.BlockSpec((tm, q_lora), lambda i: (i, col_blk)),
                  pl.BlockSpec((1, q_lora), lambda i: (0, 0)),
                  _resident((None, q_lora, heads * QK_SLAB), lambda i: (layer, 0, 0)),
                  pl.BlockSpec((1, NOPE_DIM), lambda i: (0, 0)),
                  pl.BlockSpec((tm, 2 * ROPE_DIM), lambda i: (i % nt, 0))],
        out_specs=pl.BlockSpec((heads, tm, QK_SLAB), lambda i: (0, i, 0)),
        out_shape=jax.ShapeDtypeStruct((heads, m, QK_SLAB), BF16),
        compiler_params=_params("parallel"),
        name="q_proj",
    )(proj, gq.reshape(1, -1), w_uq_r, gn.reshape(1, -1), tab)


def _kv_norm_kernel(ckv_ref, kpe_ref, g_ref, tab_ref, ckv_o, kpe_o):
    x = ckv_ref[...]
    ckv_o[...] = x * _rms(x, x.shape[-1]) * g_ref[...]
    t = kpe_ref[...]
    u = t * _rms(t, 2 * ROPE_DIM) * tab_ref[...]
    kpe_o[...] = u[:, :ROPE_DIM] + u[:, ROPE_DIM:]


def kv_norm(proj, ckv_blk, kv_lora, kpe_blk, g, tab):
    m = proj.shape[0]
    s_tab = tab.shape[0]
    tm = _tile(s_tab, 512)
    nt = s_tab // tm
    return pl.pallas_call(
        _kv_norm_kernel,
        grid=(m // tm,),
        in_specs=[pl.BlockSpec((tm, kv_lora), lambda i: (i, ckv_blk)),
                  pl.BlockSpec((tm, 2 * ROPE_DIM), lambda i: (i, kpe_blk)),
                  pl.BlockSpec((1, kv_lora), lambda i: (0, 0)),
                  pl.BlockSpec((tm, 2 * ROPE_DIM), lambda i: (i % nt, 0))],
        out_specs=[pl.BlockSpec((tm, kv_lora), lambda i: (i, 0)),
                   pl.BlockSpec((tm, ROPE_DIM), lambda i: (i, 0))],
        out_shape=[jax.ShapeDtypeStruct((m, kv_lora), F32),
                   jax.ShapeDtypeStruct((m, ROPE_DIM), F32)],
        compiler_params=_params("parallel"),
        name="kv_norm",
    )(proj, proj, g.reshape(1, -1), tab)


def _kv_up_kernel(ckv_ref, kpe_ref, w_ref, gn_ref, k_ref, v_ref, *, heads):
    xb = ckv_ref[...].astype(BF16)
    kp = kpe_ref[...]
    kp2 = jnp.concatenate([kp, kp], axis=-1).astype(BF16)
    gn = gn_ref[...]
    ones = jnp.ones((xb.shape[0], V_SLAB - V_DIM), BF16)
    for h in range(heads):
        acc = jnp.dot(xb, w_ref[:, h * 256:(h + 1) * 256], preferred_element_type=F32)
        nope = acc[:, :NOPE_DIM]
        k_ref[h, :, :NOPE_DIM] = (nope * _rms(nope, NOPE_DIM) * gn).astype(BF16)
        k_ref[h, :, NOPE_DIM:] = kp2
        v_ref[h, :, :V_DIM] = acc[:, NOPE_DIM:].astype(BF16)
        v_ref[h, :, V_DIM:] = ones


def kv_up(ckv, kpe, w_ukv, layer, gn, heads):
    m, kv_lora = ckv.shape
    tm = _tile(m, 256)
    return pl.pallas_call(
        functools.partial(_kv_up_kernel, heads=heads),
        grid=(m // tm,),
        in_specs=[pl.BlockSpec((tm, kv_lora), lambda i: (i, 0)),
                  pl.BlockSpec((tm, ROPE_DIM), lambda i: (i, 0)),
                  _resident((None, kv_lora, heads * 256), lambda i: (layer, 0, 0)),
                  pl.BlockSpec((1, NOPE_DIM), lambda i: (0, 0))],
        out_specs=[pl.BlockSpec((heads, tm, QK_SLAB), lambda i: (0, i, 0)),
                   pl.BlockSpec((heads, tm, V_SLAB), lambda i: (0, i, 0))],
        out_shape=[jax.ShapeDtypeStruct((heads, m, QK_SLAB), BF16),
                   jax.ShapeDtypeStruct((heads, m, V_SLAB), BF16)],
        compiler_params=_params("parallel"),
        name="kv_up",
    )(ckv, kpe, w_ukv, gn.reshape(1, -1))


def _flash_kernel(q_ref, k_ref, v_ref, o_ref, m_ref, acc_ref, *, heads, tq, tk, nk, q_off, unroll):
    i = pl.program_id(1)
    j = pl.program_id(2)
    q_lo = q_off + i * tq
    j_last = jnp.minimum((((q_lo + tq - 1) // CHUNK + 1) * CHUNK - 1) // tk, nk - 1)
    fully_visible = ((j * tk + tk - 1) // CHUNK) <= (q_lo // CHUNK)

    @pl.when(j == 0)
    def _():
        m_ref[...] = jnp.full(m_ref.shape, -jnp.inf, F32)
        acc_ref[...] = jnp.zeros(acc_ref.shape, F32)

    def tile(masked):
        if masked:
            q_chunk = (q_lo + lax.broadcasted_iota(jnp.int32, (tq, tk), 0)) >> CHUNK_SHIFT
            k_chunk = (j * tk + lax.broadcasted_iota(jnp.int32, (tq, tk), 1)) >> CHUNK_SHIFT
            visible = k_chunk <= q_chunk

        def head(h, carry):
            s = lax.dot_general(q_ref[h], k_ref[h], (((1,), (1,)), ((), ())),
                                preferred_element_type=F32)
            if masked:
                s = jnp.where(visible, s, -jnp.inf)
            m_prev = m_ref[h]
            m_new = jnp.maximum(m_prev, jnp.max(s, axis=-1, keepdims=True))
            alpha = jnp.exp(m_prev - m_new)
            m_b = pltpu.repeat(m_new, tk // LANE, axis=1) if tk % LANE == 0 else m_new[:, :1]
            p = jnp.exp(s - m_b)
            pv = jnp.dot(p.astype(BF16), v_ref[h], preferred_element_type=F32)
            acc_ref[h] = pltpu.repeat(alpha, V_SLAB // LANE, axis=1) * acc_ref[h] + pv
            m_ref[h] = m_new
            return carry

        lax.fori_loop(0, heads, head, 0, unroll=unroll)

    @pl.when(jnp.logical_and(j <= j_last, fully_visible))
    def _():
        tile(False)

    @pl.when(jnp.logical_and(j <= j_last, jnp.logical_not(fully_visible)))
    def _():
        tile(True)

    @pl.when(j == nk - 1)
    def _():
        for h in range(heads):
            acc = acc_ref[h]
            o_ref[:, h * V_DIM:(h + 1) * V_DIM] = (acc[:, :V_DIM] / acc[:, V_DIM:]).astype(o_ref.dtype)


def flash_attention(q, k, v, batch, *, tq=512, tk=512, unroll=16):
    heads, mq, _ = q.shape
    mk = k.shape[1]
    sq, sk = mq // batch, mk // batch
    tq = _tile(sq, tq)
    tk = _tile(sk, tk, 16)
    nq, nk = sq // tq, sk // tk
    q_off = sk - sq
    unroll = min(unroll, heads)

    def kv_index(b, i, j):
        j_last = (((q_off + i * tq + tq - 1) // CHUNK + 1) * CHUNK - 1) // tk
        return (0, b * nk + jnp.minimum(j, jnp.minimum(j_last, nk - 1)), 0)

    return pl.pallas_call(
        functools.partial(_flash_kernel, heads=heads, tq=tq, tk=tk, nk=nk, q_off=q_off, unroll=unroll),
        grid=(batch, nq, nk),
        in_specs=[pl.BlockSpec((heads, tq, QK_SLAB), lambda b, i, j: (0, b * nq + i, 0)),
                  pl.BlockSpec((heads, tk, QK_SLAB), kv_index),
                  pl.BlockSpec((heads, tk, V_SLAB), kv_index)],
        out_specs=pl.BlockSpec((tq, heads * V_DIM), lambda b, i, j: (b * nq + i, 0)),
        out_shape=jax.ShapeDtypeStruct((mq, heads * V_DIM), BF16),
        scratch_shapes=[pltpu.VMEM((heads, tq, LANE), F32),
                        pltpu.VMEM((heads, tq, V_SLAB), F32)],
        compiler_params=_params("parallel", "parallel", "arbitrary"),
        name="flash_attention",
    )(q, k, v)


def _expand_heads(x, g, rep):
    q = x.shape[0]
    halves = []
    for t in range(rep * SSM_HEADDIM // LANE):
        lane = lax.broadcasted_iota(jnp.int32, (q, LANE), 1)
        idx = g * rep + t * (LANE // SSM_HEADDIM) + (lane >> CHUNK_SHIFT)
        halves.append(jnp.take_along_axis(x, idx, axis=1))
    return jnp.concatenate(halves, axis=1)


def _ssd_kernel(xbc_ref, z_ref, dt_ref, cinit_ref, hinit_ref, cw_ref, cb_ref, dtb_ref, alog_ref,
                dsk_ref, gn_ref, bd_ref, y_ref, hout_ref, buf, ht, xs_s, bm_s, cm_s, *, groups, rep, nc):
    c = pl.program_id(1)
    q = CHUNK
    gw = rep * SSM_HEADDIM
    hp = groups * gw
    halo = CONV_W - 1
    base = 8
    conv_dim = buf.shape[1]

    @pl.when(c == 0)
    def _():
        buf[base - halo:base, :] = cinit_ref[0]
        ht[...] = hinit_ref[0]

    buf[base:base + q, :] = xbc_ref[...]
    blk = 2 * LANE
    for j in range(conv_dim // blk):
        cols = slice(j * blk, (j + 1) * blk)
        xall = buf[:, cols]
        conv = cb_ref[:, cols]
        for i in range(CONV_W):
            shifted = xall if i == halo else pltpu.roll(xall, halo - i, 0)
            conv = conv + shifted[base:base + q, :] * cw_ref[i:i + 1, cols]
        xc = _silu(conv)
        lo = j * blk
        if lo < hp:
            xs_s[:, lo:lo + blk] = xc
        elif lo < hp + groups * D_STATE:
            bm_s[:, lo - hp:lo - hp + blk] = xc.astype(BF16)
        else:
            o = lo - hp - groups * D_STATE
            cm_s[:, o:o + blk] = xc.astype(BF16)
    tail = buf[base + q - halo:base + q, :]
    buf[base - halo:base, :] = tail

    dt_in = dt_ref[...] + dtb_ref[...]
    dt = jnp.maximum(dt_in, 0.0) + jnp.log1p(jnp.exp(-jnp.abs(dt_in)))
    a = dt * (-jnp.exp(alog_ref[...]))
    row = lax.broadcasted_iota(jnp.int32, (q, q), 0)
    col = lax.broadcasted_iota(jnp.int32, (q, q), 1)
    tri = (col <= row).astype(F32)
    a_cum = jnp.dot(tri, a, precision=HIGHEST, preferred_element_type=F32)

    r_idx = lax.broadcasted_iota(jnp.int32, (q, gw), 0)
    l_idx = lax.broadcasted_iota(jnp.int32, (q, gw), 1) & (q - 1)
    on_diag = l_idx == r_idx
    causal = l_idx <= r_idx
    bd = bd_ref[...]

    for g in range(groups):
        lanes = slice(g * gw, (g + 1) * gw)
        a_col = _expand_heads(a_cum, g, rep)
        dt_col = _expand_heads(dt, g, rep)
        a_row = jnp.sum(jnp.where(on_diag, a_col, 0.0), axis=0, keepdims=True)
        decay = jnp.exp(jnp.where(causal, a_col - a_row, -jnp.inf))
        a_last = a_col[q - 1:q, :]
        xs = xs_s[:, lanes]
        xdt = xs * dt_col
        xw = (xdt * jnp.exp(a_last - a_col)).astype(BF16)
        bg = bm_s[:, g * D_STATE:(g + 1) * D_STATE]
        cg = cm_s[:, g * D_STATE:(g + 1) * D_STATE]
        b_rep = jnp.concatenate([bg] * rep, axis=0)
        cb = lax.dot_general(cg, b_rep, (((1,), (1,)), ((), ())), preferred_element_type=F32)
        mg = (cb * decay).astype(BF16)
        x_blk = jnp.concatenate([xdt.astype(BF16)] * rep, axis=0) * bd
        y_diag = jnp.dot(mg, x_blk, preferred_element_type=F32)
        h_in = ht[g]
        y_off = jnp.dot(cg, h_in.astype(BF16), preferred_element_type=F32) * jnp.exp(a_col)
        st = lax.dot_general(bg, xw, (((0,), (0,)), ((), ())), preferred_element_type=F32)
        ht[g] = h_in * jnp.exp(a_last) + st
        yg = (y_diag + y_off + dsk_ref[:, lanes] * xs) * _silu(z_ref[:, lanes])
        y_ref[:, lanes] = (yg * _rms(yg, gw) * gn_ref[:, lanes]).astype(y_ref.dtype)

    @pl.when(c == nc - 1)
    def _():
        hout_ref[0] = ht[...]


def ssd_mixer(proj, batch, xbc_blk, conv_dim, z_blk, hp, dt_blk, conv_init, h_init_t, conv_w,
              conv_b, dtb, alog, dsk_e, gn, bd_mask, groups, rep):
    m = proj.shape[0]
    nc = m // batch // CHUNK
    gw = rep * SSM_HEADDIM
    const2 = lambda b, c: (0, 0)
    return pl.pallas_call(
        functools.partial(_ssd_kernel, groups=groups, rep=rep, nc=nc),
        grid=(batch, nc),
        in_specs=[pl.BlockSpec((CHUNK, conv_dim), lambda b, c: (b * nc + c, xbc_blk)),
                  pl.BlockSpec((CHUNK, hp), lambda b, c: (b * nc + c, z_blk)),
                  pl.BlockSpec((CHUNK, LANE), lambda b, c: (b * nc + c, dt_blk)),
                  pl.BlockSpec((1, CONV_W - 1, conv_dim), lambda b, c: (b, 0, 0)),
                  pl.BlockSpec((1, groups, D_STATE, gw), lambda b, c: (b, 0, 0, 0)),
                  pl.BlockSpec((CONV_W, conv_dim), const2),
                  pl.BlockSpec((1, conv_dim), const2),
                  pl.BlockSpec((1, LANE), const2),
                  pl.BlockSpec((1, LANE), const2),
                  pl.BlockSpec((1, hp), const2),
                  pl.BlockSpec((1, hp), const2),
                  pl.BlockSpec((rep * CHUNK, gw), const2)],
        out_specs=[pl.BlockSpec((CHUNK, hp), lambda b, c: (b * nc + c, 0)),
                   pl.BlockSpec((1, groups, D_STATE, gw), lambda b, c: (b, 0, 0, 0))],
        out_shape=[jax.ShapeDtypeStruct((m, hp), BF16),
                   jax.ShapeDtypeStruct((batch, groups, D_STATE, gw), F32)],
        scratch_shapes=[pltpu.VMEM((8 + CHUNK, conv_dim), F32),
                        pltpu.VMEM((groups, D_STATE, gw), F32),
                        pltpu.VMEM((CHUNK, hp), F32),
                        pltpu.VMEM((CHUNK, groups * D_STATE), BF16),
                        pltpu.VMEM((CHUNK, groups * D_STATE), BF16)],
        compiler_params=_params("parallel", "arbitrary"),
        name="ssd_mixer",
    )(proj, proj, proj, conv_init, h_init_t, conv_w, conv_b, dtb, alog, dsk_e, gn, bd_mask)


def _mem_kv_kernel(x_ref, g_ref, w_ref, gk_ref, k_ref, v_ref, *, heads):
    x = x_ref[...]
    xn = (x * _rms(x, x.shape[-1]) * g_ref[...]).astype(BF16)
    width = heads * XA_HEAD_DIM
    kv = jnp.dot(xn, w_ref[...], preferred_element_type=F32)
    for h in range(heads):
        kh = kv[:, h * XA_HEAD_DIM:(h + 1) * XA_HEAD_DIM]
        k_ref[:, h * XA_HEAD_DIM:(h + 1) * XA_HEAD_DIM] = kh * _rms(kh, XA_HEAD_DIM) * gk_ref[...]
    v_ref[...] = kv[:, width:]


def memory_kv(mem, g, w_kv, layer, gk, heads):
    m, d = mem.shape
    width = heads * XA_HEAD_DIM
    tm = _tile(m, 256)
    return pl.pallas_call(
        functools.partial(_mem_kv_kernel, heads=heads),
        grid=(m // tm,),
        in_specs=[pl.BlockSpec((tm, d), lambda i: (i, 0)),
                  pl.BlockSpec((1, d), lambda i: (0, 0)),
                  _resident((None, d, 2 * width), lambda i: (layer, 0, 0)),
                  pl.BlockSpec((1, XA_HEAD_DIM), lambda i: (0, 0))],
        out_specs=[pl.BlockSpec((tm, width), lambda i: (i, 0)),
                   pl.BlockSpec((tm, width), lambda i: (i, 0))],
        out_shape=[jax.ShapeDtypeStruct((m, width), F32), jax.ShapeDtypeStruct((m, width), F32)],
        compiler_params=_params("parallel"),
        name="memory_kv",
    )(mem, g.reshape(1, d), w_kv, gk.reshape(1, -1))


def _xattn_kernel(x_ref, gxa_ref, wq_ref, gq_ref, mk_ref, mv_ref, wo_ref, gffn_ref, xo_ref, ho_ref,
                  *, heads):
    x = x_ref[...]
    d = x.shape[-1]
    xn = (x * _rms(x, d) * gxa_ref[...]).astype(BF16)
    q = jnp.dot(xn, wq_ref[...], preferred_element_type=F32)
    outs = []
    for h in range(heads):
        cols = slice(h * XA_HEAD_DIM, (h + 1) * XA_HEAD_DIM)
        qh = q[:, cols]
        qh = (qh * _rms(qh, XA_HEAD_DIM) * gq_ref[...] * XA_SCALE).astype(BF16)
        kh = mk_ref[0, :, cols].astype(BF16)
        vh = mv_ref[0, :, cols].astype(BF16)
        s = lax.dot_general(qh, kh, (((1,), (1,)), ((), ())), preferred_element_type=F32)
        p = jnp.exp(s - jnp.max(s, axis=-1, keepdims=True))
        p = p / jnp.sum(p, axis=-1, keepdims=True)
        outs.append(jnp.dot(p.astype(BF16), vh, preferred_element_type=F32))
    o = jnp.concatenate(outs, axis=-1).astype(BF16)
    x_new = x + jnp.dot(o, wo_ref[...], preferred_element_type=F32)
    xo_ref[...] = x_new
    ho_ref[...] = (x_new * _rms(x_new, d) * gffn_ref[...]).astype(ho_ref.dtype)


def cross_attention_block(x, batch, gxa, wq, gq, mem_k, mem_v, wo, layer, gffn, heads):
    m, d = x.shape
    s = m // batch
    n_mem = mem_k.shape[1]
    width = heads * XA_HEAD_DIM
    tm = _tile(s, 256)
    per_b = s // tm
    return pl.pallas_call(
        functools.partial(_xattn_kernel, heads=heads),
        grid=(m // tm,),
        in_specs=[pl.BlockSpec((tm, d), lambda i: (i, 0)),
                  pl.BlockSpec((1, d), lambda i: (0, 0)),
                  _resident((None, d, width), lambda i: (layer, 0, 0)),
                  pl.BlockSpec((1, XA_HEAD_DIM), lambda i: (0, 0)),
                  pl.BlockSpec((1, n_mem, width), lambda i: (i // per_b, 0, 0)),
                  pl.BlockSpec((1, n_mem, width), lambda i: (i // per_b, 0, 0)),
                  _resident((None, width, d), lambda i: (layer, 0, 0)),
                  pl.BlockSpec((1, d), lambda i: (0, 0))],
        out_specs=[pl.BlockSpec((tm, d), lambda i: (i, 0)),
                   pl.BlockSpec((tm, d), lambda i: (i, 0))],
        out_shape=[jax.ShapeDtypeStruct((m, d), F32), jax.ShapeDtypeStruct((m, d), BF16)],
        compiler_params=_params("parallel"),
        name="cross_attention_block",
    )(x, gxa.reshape(1, d), wq, gq.reshape(1, -1), mem_k, mem_v, wo, gffn.reshape(1, d))


def _layout(q_lora, kv_lora, ssm_inner, conv_dim):
    off = {}
    off["cq"] = 0
    off["ckv"] = _round_up(q_lora, kv_lora)
    off["kpe"] = _round_up(off["ckv"] + kv_lora, LANE)
    off["dt"] = off["kpe"] + LANE
    off["z"] = _round_up(off["dt"] + LANE, ssm_inner)
    off["xbc"] = _round_up(off["z"] + ssm_inner, conv_dim)
    off["total"] = _round_up(off["xbc"] + conv_dim, 1024)
    return off


_SWAP_HALVES = np.concatenate([np.arange(ROPE_DIM // 2, ROPE_DIM), np.arange(ROPE_DIM // 2)])


def _prep_weights(w, dims):
    q_lora, kv_lora, heads, ssm_heads, hp, conv_dim = (dims[k] for k in
                                                       ("q_lora", "kv_lora", "heads", "ssm_heads", "hp", "conv_dim"))
    off = dims["off"]
    w_in = w["w_in"]
    depth, d, _ = w_in.shape
    o_kpe = q_lora + kv_lora
    o_z = o_kpe + ROPE_DIM
    o_xbc = o_z + hp
    o_dt = o_xbc + conv_dim
    kpe = w_in[..., o_kpe:o_kpe + ROPE_DIM]
    segments = [("cq", w_in[..., :q_lora]), ("ckv", w_in[..., q_lora:o_kpe]),
                ("kpe", jnp.concatenate([kpe, kpe[..., _SWAP_HALVES]], axis=-1)),
                ("dt", w_in[..., o_dt:o_dt + ssm_heads]), ("z", w_in[..., o_z:o_xbc]),
                ("xbc", w_in[..., o_xbc:o_dt])]
    parts, cursor = [], 0
    for name, seg in segments:
        if off[name] > cursor:
            parts.append(jnp.zeros((depth, d, off[name] - cursor), w_in.dtype))
        parts.append(seg)
        cursor = off[name] + seg.shape[-1]
    if off["total"] > cursor:
        parts.append(jnp.zeros((depth, d, off["total"] - cursor), w_in.dtype))
    p = {"w_in": jnp.concatenate(parts, axis=-1).astype(BF16)}

    qk = NOPE_DIM + ROPE_DIM
    uq_idx = np.concatenate([np.concatenate([h * qk + np.arange(qk), h * qk + NOPE_DIM + _SWAP_HALVES])
                             for h in range(heads)])
    p["w_uq"] = jnp.take(w["w_uq"], jnp.asarray(uq_idx, np.int32), axis=2).astype(BF16)
    p["w_ukv"] = w["w_ukv"].astype(BF16)
    p["w_o"] = w["w_o"].astype(BF16)
    p["w_xq"] = w["w_xq"].astype(BF16)
    p["w_xkv"] = jnp.concatenate([w["w_xk"], w["w_xv"]], axis=-1).astype(BF16)
    p["w_xo"] = w["w_xo"].astype(BF16)
    p["w_down"] = w["w_down"].astype(BF16)

    pad = LANE - ssm_heads
    p["dt_bias"] = jnp.pad(w["dt_bias"], ((0, 0), (0, pad)))
    p["a_log"] = jnp.pad(w["a_log"], ((0, 0), (0, pad)))
    p["d_skip_e"] = jnp.repeat(w["d_skip"], SSM_HEADDIM, axis=-1)
    return p


def _rope_table(pos, gain):
    half = ROPE_DIM // 2
    inv = ROPE_BASE ** (-jnp.arange(half, dtype=F32) / half)
    ang = pos.astype(F32)[:, None] * inv[None, :]
    cos, sin = jnp.cos(ang), jnp.sin(ang)
    g1, g2 = gain[:half][None, :], gain[half:][None, :]
    return jnp.concatenate([g1 * cos, g2 * cos, -g2 * sin, g1 * sin], axis=-1)


def _block(x, batch, pos, past_ckv, past_kpe, conv_init, h_init_t, mem_k, mem_v, l, w, p, dims):
    m, d = x.shape
    s = m // batch
    heads, q_lora, kv_lora, hp, conv_dim = (dims[k] for k in ("heads", "q_lora", "kv_lora", "hp", "conv_dim"))
    off = dims["off"]

    h = rmsnorm_bf16(x, w["norm_mix"][l])
    proj = matmul(h, p["w_in"], l, out_dtype=F32)

    tab_q = _rope_table(pos, w["qn_pe"][l])
    tab_k = _rope_table(pos, w["kn_pe"][l])
    q = q_proj(proj, off["cq"] // q_lora, q_lora, w["q_norm"][l], p["w_uq"], l, w["qn_nope"][l], tab_q, heads)
    ckv_new, kpe_new = kv_norm(proj, off["ckv"] // kv_lora, kv_lora, off["kpe"] // LANE,
                               w["kv_norm"][l], tab_k)
    if past_ckv is None:
        ckv_all, kpe_all = ckv_new, kpe_new
    else:
        ckv_all = jnp.concatenate([past_ckv, ckv_new.reshape(batch, s, kv_lora)], axis=1).reshape(-1, kv_lora)
        kpe_all = jnp.concatenate([past_kpe, kpe_new.reshape(batch, s, ROPE_DIM)], axis=1).reshape(-1, ROPE_DIM)
    k, v = kv_up(ckv_all, kpe_all, p["w_ukv"], l, w["kn_nope"][l], heads)
    attn = flash_attention(q, k, v, batch)

    y, h_new_t = ssd_mixer(proj, batch, off["xbc"] // conv_dim, conv_dim, off["z"] // hp, hp,
                           off["dt"] // LANE, conv_init, h_init_t, w["conv_w"][l],
                           w["conv_b"][l].reshape(1, -1), p["dt_bias"][l].reshape(1, -1),
                           p["a_log"][l].reshape(1, -1), p["d_skip_e"][l].reshape(1, -1),
                           w["ssm_norm"][l].reshape(1, -1), dims["bd_mask"], dims["groups"], dims["rep"])
    conv_new = proj.reshape(batch, s, -1)[:, s - (CONV_W - 1):, off["xbc"]:off["xbc"] + conv_dim]

    x = matmul_concat_residual(attn, y, p["w_o"], l, x)
    x, h_ffn = cross_attention_block(x, batch, w["norm_xa"][l], p["w_xq"], w["xa_norm_q"][l],
                                     mem_k, mem_v, p["w_xo"], l, w["norm_ffn"][l], dims["xa_heads"])
    hid = swiglu_up(h_ffn, w["w_gate"], w["w_up"], l)
    f = hid.shape[1]
    x = matmul(hid, p["w_down"], l, out_dtype=F32, residual=x, tm=1024, tn=512,
               tk=f // 2 if (f // 2) % LANE == 0 else None)
    return x, ckv_new, kpe_new, conv_new, h_new_t


def _state_to_t(h, groups, rep):
    b = h.shape[0]
    return h.reshape(b, groups, rep * SSM_HEADDIM, D_STATE).transpose(0, 1, 3, 2)


def _state_from_t(ht, groups, rep):
    b = ht.shape[0]
    return ht.transpose(0, 1, 3, 2).reshape(b, groups * rep, SSM_HEADDIM, D_STATE)


def kernel(x_prompt, x_sample, mem_prompt, cache_ckv, cache_kpe, state_conv, state_ssm, cache_mem_k, cache_mem_v, norm_mix, w_in, q_norm, w_uq, kv_norm, w_ukv, qn_nope, qn_pe, kn_nope, kn_pe, conv_w, conv_b, dt_bias, a_log, d_skip, ssm_norm, w_o, norm_xa, mem_norm, w_xq, w_xk, w_xv, xa_norm_q, xa_norm_k, w_xo, norm_ffn, w_gate, w_up, w_down):
    w = dict(norm_mix=norm_mix, w_in=w_in, q_norm=q_norm, w_uq=w_uq, kv_norm=kv_norm, w_ukv=w_ukv,
             qn_nope=qn_nope, qn_pe=qn_pe, kn_nope=kn_nope, kn_pe=kn_pe, conv_w=conv_w, conv_b=conv_b,
             dt_bias=dt_bias, a_log=a_log, d_skip=d_skip, ssm_norm=ssm_norm, w_o=w_o, norm_xa=norm_xa,
             mem_norm=mem_norm, w_xq=w_xq, w_xk=w_xk, w_xv=w_xv, xa_norm_q=xa_norm_q,
             xa_norm_k=xa_norm_k, w_xo=w_xo, norm_ffn=norm_ffn, w_gate=w_gate, w_up=w_up, w_down=w_down)
    bp, sp, d = x_prompt.shape
    bs, ss, _ = x_sample.shape
    depth = w_in.shape[0]
    past = cache_ckv.shape[2]
    q_lora, kv_lora = q_norm.shape[-1], kv_norm.shape[-1]
    heads = w_uq.shape[-1] // (NOPE_DIM + ROPE_DIM)
    ssm_heads, hp, conv_dim = dt_bias.shape[-1], ssm_norm.shape[-1], conv_w.shape[-1]
    groups = (conv_dim - hp) // (2 * D_STATE)
    rep = ssm_heads // groups
    xa_heads = w_xq.shape[-1] // XA_HEAD_DIM
    n_mem = mem_prompt.shape[1]
    assert ssm_heads <= LANE and hp == ssm_heads * SSM_HEADDIM and sp % CHUNK == 0 and ss % CHUNK == 0

    blocks = np.arange(rep * CHUNK) // CHUNK
    bd_mask = (blocks[:, None] == (np.arange(rep * SSM_HEADDIM) // SSM_HEADDIM)[None, :]).astype(np.float32)
    dims = dict(q_lora=q_lora, kv_lora=kv_lora, heads=heads, ssm_heads=ssm_heads, hp=hp,
                conv_dim=conv_dim, groups=groups, rep=rep, xa_heads=xa_heads,
                off=_layout(q_lora, kv_lora, hp, conv_dim), bd_mask=jnp.asarray(bd_mask, BF16))
    p = _prep_weights(w, dims)

    pos_p = jnp.arange(sp, dtype=jnp.int32)
    pos_s = past + jnp.arange(ss, dtype=jnp.int32)
    conv0 = jnp.zeros((bp, CONV_W - 1, conv_dim), F32)
    ssm0_t = jnp.zeros((bp, groups, D_STATE, rep * SSM_HEADDIM), F32)

    hp_x = x_prompt.reshape(bp * sp, d)
    hs_x = x_sample.reshape(bs * ss, d)
    mem2d = mem_prompt.reshape(bp * n_mem, d)
    outs = {k: [] for k in ("ckv_p", "kpe_p", "conv_p", "ssm_p", "mk_p", "mv_p",
                            "ckv_s", "kpe_s", "conv_s", "ssm_s")}
    for l in range(depth):
        mk, mv = memory_kv(mem2d, mem_norm[l], p["w_xkv"], l, xa_norm_k[l], xa_heads)
        mk = mk.reshape(bp, n_mem, -1)
        mv = mv.reshape(bp, n_mem, -1)
        hp_x, a, b, c, dd = _block(hp_x, bp, pos_p, None, None, conv0, ssm0_t, mk, mv, l, w, p, dims)
        outs["ckv_p"].append(a.reshape(bp, sp, kv_lora))
        outs["kpe_p"].append(b.reshape(bp, sp, ROPE_DIM))
        outs["conv_p"].append(c)
        outs["ssm_p"].append(_state_from_t(dd, groups, rep))
        outs["mk_p"].append(mk.reshape(bp, n_mem, xa_heads, XA_HEAD_DIM))
        outs["mv_p"].append(mv.reshape(bp, n_mem, xa_heads, XA_HEAD_DIM))
        hs_x, a, b, c, dd = _block(hs_x, bs, pos_s, cache_ckv[l], cache_kpe[l], state_conv[l],
                                   _state_to_t(state_ssm[l], groups, rep),
                                   cache_mem_k[l].reshape(bs, n_mem, -1),
                                   cache_mem_v[l].reshape(bs, n_mem, -1), l, w, p, dims)
        outs["ckv_s"].append(a.reshape(bs, ss, kv_lora))
        outs["kpe_s"].append(b.reshape(bs, ss, ROPE_DIM))
        outs["conv_s"].append(c)
        outs["ssm_s"].append(_state_from_t(dd, groups, rep))

    st = lambda k: jnp.stack(outs[k])
    return (hp_x.reshape(bp, sp, d), hs_x.reshape(bs, ss, d),
            st("ckv_p"), st("kpe_p"), st("conv_p"), st("ssm_p"), st("mk_p"), st("mv_p"),
            st("ckv_s"), st("kpe_s"), st("conv_s"), st("ssm_s"))
```
